```python
import jax, jax.numpy as jnp
from jax import lax
import numpy as np

D_MODEL = 1024
BATCH = 8
SEQ = 2048
DEPTH = 2
DEC_BATCH = 128
DEC_SEQ = 8
PAST_LEN = 16384
PAGE_SIZE = 128

D_MIX = D_MODEL
N_MIXERS = 4
GROUP = D_MIX // N_MIXERS
HEADS_PER_MIXER = 4
HEAD_DIM = GROUP // HEADS_PER_MIXER
N_IN_SLICES = 12
CONV_A_WIDTH = 31
CONV_B_WIDTH = 3
CHUNK = 128
POOL_WINDOWS = (2, 4, 8, 16)
POOL_BUF = max(POOL_WINDOWS) - 1
EPS = 1e-6

kernel_name = "hybrid_conv_gmlp_pool_decoder_step"


def _rmsnorm(x, g):
    xf = x.astype(jnp.float32)
    y = xf * lax.rsqrt(jnp.mean(xf * xf, axis=-1, keepdims=True) + EPS)
    return (y * g.astype(jnp.float32)).astype(x.dtype)


def _layernorm(x, g, b):
    xf = x.astype(jnp.float32)
    mu = jnp.mean(xf, axis=-1, keepdims=True)
    var = jnp.mean(jnp.square(xf - mu), axis=-1, keepdims=True)
    y = (xf - mu) * lax.rsqrt(var + EPS)
    return (y * g.astype(jnp.float32) + b.astype(jnp.float32)).astype(x.dtype)


def _depthwise_valid(ext, w):
    return lax.conv_general_dilated(ext, w[:, None, :].astype(ext.dtype), window_strides=(1,), padding='VALID',
                                    dimension_numbers=('NWC', 'WIO', 'NWC'), feature_group_count=ext.shape[-1])


def _chunk_spatial(v, w_s, b_s):
    bsz, L, _ = v.shape
    n_chunks = -(-L // CHUNK)
    vp = jnp.pad(v, ((0, 0), (0, n_chunks * CHUNK - L), (0, 0)))
    vr = vp.reshape(bsz, n_chunks, CHUNK, HEADS_PER_MIXER, HEAD_DIM)
    mask = jnp.tril(jnp.ones((CHUNK, CHUNK), dtype=w_s.dtype))
    s = jnp.einsum('hts,bnshc->bnthc', w_s * mask, vr) + b_s.T[None, None, :, :, None]
    return s.reshape(bsz, n_chunks * CHUNK, GROUP)[:, :L]


def _pool_mixer(xd, buf, pos0, w_p, scale):
    bsz, L, _ = xd.shape
    ext = jnp.concatenate([buf, xd], axis=1)
    ext32 = ext.astype(jnp.float32)
    cs = jnp.pad(jnp.cumsum(ext32, axis=1), ((0, 0), (1, 0), (0, 0)))
    pos = pos0 + jnp.arange(L)
    pooled = []
    for g, w in enumerate(POOL_WINDOWS):
        sl = slice(g * HEAD_DIM, (g + 1) * HEAD_DIM)
        end = cs[:, POOL_BUF + 1:POOL_BUF + 1 + L, sl]
        begin = cs[:, POOL_BUF + 1 - w:POOL_BUF + 1 - w + L, sl]
        cnt = jnp.minimum(pos + 1, w).astype(jnp.float32)[None, :, None]
        pooled.append((end - begin) / cnt)
    pooled = jnp.concatenate(pooled, axis=-1) - ext32[:, POOL_BUF:]
    pr = pooled.reshape(bsz, L, HEADS_PER_MIXER, HEAD_DIM)
    out = jnp.einsum('blgc,gcd->blgd', pr, w_p.astype(jnp.float32)).reshape(bsz, L, GROUP)
    out = out * scale.astype(jnp.float32)
    return out.astype(xd.dtype), ext[:, -POOL_BUF:]


def _layer(x, buf_a, buf_b, buf_d, pos0, pre_g, w_in, ca_w, ca_b, lna_g, lna_b, cb_w,
           lnc_g, lnc_b, sp_w, sp_b, pl_w, pl_s, w_out, post_g):
    h = _rmsnorm(x, pre_g)
    proj = jnp.einsum('bld,de->ble', h, w_in)
    (a_val, a_gate, z_a, b_b, b_c, b_x, z_b, c_u, c_v, z_c, d_x, z_d) = jnp.split(proj, N_IN_SLICES, axis=-1)
    glu = a_val * jax.nn.sigmoid(a_gate)
    ext_a = jnp.concatenate([buf_a, glu], axis=1)
    ya = _depthwise_valid(ext_a, ca_w) + ca_b
    ya = jax.nn.silu(_layernorm(ya, lna_g, lna_b)) * jax.nn.silu(z_a)
    hb = b_c * b_x
    ext_b = jnp.concatenate([buf_b, hb], axis=1)
    yb = b_b * _depthwise_valid(ext_b, cb_w) * jax.nn.silu(z_b)
    vn = _layernorm(c_v, lnc_g, lnc_b)
    yc = c_u * _chunk_spatial(vn, sp_w, sp_b) * jax.nn.silu(z_c)
    yd, new_buf_d = _pool_mixer(d_x, buf_d, pos0, pl_w, pl_s)
    yd = yd * jax.nn.silu(z_d)
    mix = jnp.concatenate([ya, yb, yc, yd], axis=-1)
    out = jnp.einsum('ble,ed->bld', mix, w_out)
    y = x + _rmsnorm(out, post_g)
    return y, ext_a[:, -(CONV_A_WIDTH - 1):], ext_b[:, -(CONV_B_WIDTH - 1):], new_buf_d, vn


def setup_inputs(seed: int = 0) -> dict:
    key = jax.random.key(seed)
    ks = jax.random.split(key, 24)
    nrm = lambda k, s: jax.random.normal(k, s, dtype=jnp.float32)
    return {
        "x_prompt": nrm(ks[0], (BATCH, SEQ, D_MODEL)),
        "x_sample": nrm(ks[1], (DEC_BATCH, DEC_SEQ, D_MODEL)),
        "state_conv_a": 0.5 * nrm(ks[2], (DEPTH, DEC_BATCH, CONV_A_WIDTH - 1, GROUP)),
        "state_conv_b": 0.5 * nrm(ks[3], (DEPTH, DEC_BATCH, CONV_B_WIDTH - 1, GROUP)),
        "state_pool": nrm(ks[4], (DEPTH, DEC_BATCH, POOL_BUF, GROUP)),
        "pre_norm_g": 1.0 + 0.02 * nrm(ks[5], (DEPTH, D_MODEL)),
        "w_in": nrm(ks[6], (DEPTH, D_MODEL, N_IN_SLICES * GROUP)) * D_MODEL ** -0.5,
        "conv_a_w": nrm(ks[7], (DEPTH, CONV_A_WIDTH, GROUP)) * CONV_A_WIDTH ** -0.5,
        "conv_a_b": 0.02 * nrm(ks[8], (DEPTH, GROUP)),
        "ln_a_g": 1.0 + 0.02 * nrm(ks[9], (DEPTH, GROUP)),
        "ln_a_b": 0.02 * nrm(ks[10], (DEPTH, GROUP)),
        "conv_b_w": nrm(ks[11], (DEPTH, CONV_B_WIDTH, GROUP)) * CONV_B_WIDTH ** -0.5,
        "ln_c_g": 1.0 + 0.02 * nrm(ks[12], (DEPTH, GROUP)),
        "ln_c_b": 0.02 * nrm(ks[13], (DEPTH, GROUP)),
        "spatial_w": nrm(ks[14], (DEPTH, HEADS_PER_MIXER, CHUNK, CHUNK)) * CHUNK ** -0.5,
        "spatial_b": 1.0 + 0.02 * nrm(ks[15], (DEPTH, HEADS_PER_MIXER, CHUNK)),
        "pool_w": nrm(ks[16], (DEPTH, HEADS_PER_MIXER, HEAD_DIM, HEAD_DIM)) * HEAD_DIM ** -0.5,
        "pool_scale": 1.0 + 0.02 * nrm(ks[17], (DEPTH, GROUP)),
        "w_out": nrm(ks[18], (DEPTH, D_MIX, D_MODEL)) * D_MIX ** -0.5,
        "post_norm_g": 1.0 + 0.02 * nrm(ks[19], (DEPTH, D_MODEL)),
    }


def reference(x_prompt, x_sample, state_conv_a, state_conv_b, state_pool, pre_norm_g, w_in, conv_a_w, conv_a_b,
              ln_a_g, ln_a_b, conv_b_w, ln_c_g, ln_c_b, spatial_w, spatial_b, pool_w, pool_scale, w_out,
              post_norm_g):
    bp = x_prompt.shape[0]
    dt = x_prompt.dtype
    zero_a = jnp.zeros((bp, CONV_A_WIDTH - 1, GROUP), dt)
    zero_b = jnp.zeros((bp, CONV_B_WIDTH - 1, GROUP), dt)
    zero_d = jnp.zeros((bp, POOL_BUF, GROUP), dt)
    yp, ys = x_prompt, x_sample
    ca_p, ca_s, cb_p, cb_s, pd_p, pd_s, cv_s = [], [], [], [], [], [], []
    for l in range(DEPTH):
        params = (pre_norm_g[l], w_in[l], conv_a_w[l], conv_a_b[l], ln_a_g[l], ln_a_b[l], conv_b_w[l],
                  ln_c_g[l], ln_c_b[l], spatial_w[l], spatial_b[l], pool_w[l], pool_scale[l], w_out[l],
                  post_norm_g[l])
        yp, a_p, b_p, d_p, _ = _layer(yp, zero_a, zero_b, zero_d, 0, *params)
        ys, a_s, b_s, d_s, v_s = _layer(ys, state_conv_a[l], state_conv_b[l], state_pool[l], PAST_LEN, *params)
        ca_p.append(a_p); ca_s.append(a_s); cb_p.append(b_p); cb_s.append(b_s)
        pd_p.append(d_p); pd_s.append(d_s); cv_s.append(v_s)
    return (yp, ys, jnp.stack(ca_p), jnp.stack(ca_s), jnp.stack(cb_p), jnp.stack(cb_s),
            jnp.stack(pd_p), jnp.stack(pd_s), jnp.stack(cv_s))
```

```python
import functools

import jax
import jax.numpy as jnp
from jax import lax
from jax.experimental import pallas as pl
from jax.experimental.pallas import tpu as pltpu

D_MODEL = 1024
GROUP = 256
HEAD_DIM = 64
N_HEADS = 4
CONV_A_WIDTH = 31
CONV_B_WIDTH = 3
CHUNK = 128
POOL_BUF = 15
EPS = 1e-6
LANES = 128
HALVES = GROUP // LANES

(A_VAL, A_GATE, Z_A, B_B, B_C, B_X, Z_B, C_U, C_V, Z_C, D_X, Z_D) = range(12)
N_IN = 12 * GROUP
SPATIAL_OUT = C_V
POOL_OUT = D_X

PAST_A = 32
PAST_B = 8
PAST_D = 16

PROMPT_TILE = 256
SAMPLE_BLOCK = 32
ROWS = 32

VMEM_LIMIT = 48 * 1024 * 1024


def _sigmoid(v):
    return 1.0 / (1.0 + jnp.exp(-v))


def _silu(v):
    return v * _sigmoid(v)


def _rowsum(vs):
    tot = vs[0]
    for v in vs[1:]:
        tot = tot + v
    return jnp.sum(tot, axis=-1, keepdims=True)


def _half(h):
    return slice(h * LANES, (h + 1) * LANES)


def _layernorm2(v0, v1, g_ref, b_ref):
    mu = _rowsum([v0, v1]) * (1.0 / GROUP)
    d0, d1 = v0 - mu, v1 - mu
    var = _rowsum([d0 * d0, d1 * d1]) * (1.0 / GROUP)
    inv = lax.rsqrt(var + EPS)
    return (d0 * inv * g_ref[:, _half(0)] + b_ref[:, _half(0)],
            d1 * inv * g_ref[:, _half(1)] + b_ref[:, _half(1)])


def _layer_body(x_ref, pre_g, w_in, ca_w, ca_b, lna_g, lna_b, cb_w, lnc_g, lnc_b, pl_w, pl_s,
                w_out, post_g, y_ref, ext_a, ext_b, ext_d, proj, vn_ref, mix, *,
                n_rows, stride, spatial_prepare, spatial_fn, first_block_cnt):
    cur_a, cur_b, cur_d = PAST_A * stride, PAST_B * stride, PAST_D * stride
    lane = lax.broadcasted_iota(jnp.int32, (ROWS, LANES), 1)
    lo = lane < HEAD_DIM
    inv_w = [jnp.where(lo, 1.0 / 2, 1.0 / 4), jnp.where(lo, 1.0 / 8, 1.0 / 16)]
    win = [jnp.where(lo, 2, 4), jnp.where(lo, 8, 16)]

    def pj(r0, slot, h):
        c0 = slot * GROUP + h * LANES
        return proj[r0:r0 + ROWS, c0:c0 + LANES]

    def mix_cols(slot, h):
        return slice(slot * GROUP + h * LANES, slot * GROUP + (h + 1) * LANES)

    for r0 in range(0, n_rows, ROWS):
        xb = x_ref[r0:r0 + ROWS, :]
        ms = jnp.sum(xb * xb, axis=-1, keepdims=True) * (1.0 / D_MODEL)
        hb = xb * lax.rsqrt(ms + EPS) * pre_g[...]
        mix[r0:r0 + ROWS, :] = hb.astype(jnp.bfloat16)
    proj[...] = jnp.dot(mix[...], w_in[...], preferred_element_type=jnp.float32)

    for r0 in range(0, n_rows, ROWS):
        for h in range(HALVES):
            ext_a[h, cur_a + r0:cur_a + r0 + ROWS, :] = pj(r0, A_VAL, h) * _sigmoid(pj(r0, A_GATE, h))
            ext_b[h, cur_b + r0:cur_b + r0 + ROWS, :] = pj(r0, B_C, h) * pj(r0, B_X, h)
            ext_d[h, cur_d + r0:cur_d + r0 + ROWS, :] = pj(r0, D_X, h)
        v0, v1 = _layernorm2(pj(r0, C_V, 0), pj(r0, C_V, 1), lnc_g, lnc_b)
        vn_ref[r0:r0 + ROWS, _half(0)] = v0
        vn_ref[r0:r0 + ROWS, _half(1)] = v1

        def tap(h, j, extra=0):
            off = cur_d - j * stride + r0 - extra
            return ext_d[h, off:off + ROWS + extra, :]

        a2 = tap(0, 0) + tap(0, 1)
        a4 = a2 + (tap(0, 2) + tap(0, 3))
        sums = [jnp.where(lo, a2, a4)]
        if stride == 1:
            s8 = tap(1, 0, 8)
            for j in range(1, 8):
                s8 = s8 + tap(1, j, 8)
            s8_cur, s8_prev = s8[8:], s8[:ROWS]
        else:
            s8_cur, s8_prev = tap(1, 0), tap(1, 8)
            for j in range(1, 8):
                s8_cur = s8_cur + tap(1, j)
                s8_prev = s8_prev + tap(1, 8 + j)
        sums.append(jnp.where(lo, s8_cur, s8_cur + s8_prev))
        cnt = first_block_cnt() if r0 == 0 else None
        for h in range(HALVES):
            if cnt is None:
                mean = sums[h] * inv_w[h]
            else:
                mean = sums[h] / jnp.minimum(cnt, win[h]).astype(jnp.float32)
            mix[r0:r0 + ROWS, mix_cols(3, h)] = (mean - tap(h, 0)).astype(jnp.bfloat16)

    spatial_prepare()
    proj[:, POOL_OUT * GROUP:(POOL_OUT + 1) * GROUP] = jnp.dot(
        mix[:, 3 * GROUP:4 * GROUP], pl_w[...], preferred_element_type=jnp.float32)

    for r0 in range(0, n_rows, ROWS):
        ya = []
        for h in range(HALVES):
            acc = jnp.broadcast_to(ca_b[:, _half(h)], (ROWS, LANES))
            for k in range(CONV_A_WIDTH):
                off = cur_a - (CONV_A_WIDTH - 1 - k) * stride + r0
                acc = acc + ca_w[k:k + 1, _half(h)] * ext_a[h, off:off + ROWS, :]
            ya.append(acc)
        na = _layernorm2(ya[0], ya[1], lna_g, lna_b)
        for h in range(HALVES):
            mix[r0:r0 + ROWS, mix_cols(0, h)] = (_silu(na[h]) * _silu(pj(r0, Z_A, h))).astype(jnp.bfloat16)

        for h in range(HALVES):
            acc = None
            for k in range(CONV_B_WIDTH):
                off = cur_b - (CONV_B_WIDTH - 1 - k) * stride + r0
                term = cb_w[k:k + 1, _half(h)] * ext_b[h, off:off + ROWS, :]
                acc = term if acc is None else acc + term
            yb = pj(r0, B_B, h) * acc * _silu(pj(r0, Z_B, h))
            mix[r0:r0 + ROWS, mix_cols(1, h)] = yb.astype(jnp.bfloat16)

        s = spatial_fn(r0)
        for h in range(HALVES):
            yc = pj(r0, C_U, h) * s[h] * _silu(pj(r0, Z_C, h))
            mix[r0:r0 + ROWS, mix_cols(2, h)] = yc.astype(jnp.bfloat16)

        for h in range(HALVES):
            yd = pj(r0, POOL_OUT, h) * pl_s[:, _half(h)] * _silu(pj(r0, Z_D, h))
            mix[r0:r0 + ROWS, mix_cols(3, h)] = yd.astype(jnp.bfloat16)

    proj[:, 0:D_MODEL] = jnp.dot(mix[...], w_out[...], preferred_element_type=jnp.float32)
    for r0 in range(0, n_rows, ROWS):
        ob = proj[r0:r0 + ROWS, 0:D_MODEL]
        ms = jnp.sum(ob * ob, axis=-1, keepdims=True) * (1.0 / D_MODEL)
        y_ref[r0:r0 + ROWS, :] = x_ref[r0:r0 + ROWS, :] + ob * lax.rsqrt(ms + EPS) * post_g[...]


def _prompt_kernel(x_ref, pre_g, w_in, ca_w, ca_b, lna_g, lna_b, cb_w, lnc_g, lnc_b, sp_w, sp_b,
                   pl_w, pl_s, w_out, post_g,
                   y_ref, ca_out, cb_out, pd_out,
                   ext_a, ext_b, ext_d, proj, vn_ref, mix):
    tile = PROMPT_TILE
    j = pl.program_id(1)

    @pl.when(j == 0)
    def _():
        ext_a[:, 0:PAST_A, :] = jnp.zeros((HALVES, PAST_A, LANES), jnp.float32)
        ext_b[:, 0:PAST_B, :] = jnp.zeros((HALVES, PAST_B, LANES), jnp.float32)
        ext_d[:, 0:PAST_D, :] = jnp.zeros((HALVES, PAST_D, LANES), jnp.float32)

    @pl.when(j > 0)
    def _():
        ext_a[:, 0:PAST_A, :] = ext_a[:, tile:tile + PAST_A, :]
        ext_b[:, 0:PAST_B, :] = ext_b[:, tile:tile + PAST_B, :]
        ext_d[:, 0:PAST_D, :] = ext_d[:, tile:tile + PAST_D, :]

    lo = lax.broadcasted_iota(jnp.int32, (CHUNK, LANES), 1) < HEAD_DIM

    def spatial_prepare():
        for c0 in range(0, tile, CHUNK):
            for h in range(HALVES):
                vb = vn_ref[c0:c0 + CHUNK, _half(h)].astype(jnp.bfloat16)
                s_even = jnp.dot(sp_w[2 * h], vb, preferred_element_type=jnp.float32)
                s_odd = jnp.dot(sp_w[2 * h + 1], vb, preferred_element_type=jnp.float32)
                c = SPATIAL_OUT * GROUP + h * LANES
                proj[c0:c0 + CHUNK, c:c + LANES] = jnp.where(lo, s_even, s_odd) + sp_b[:, _half(h)]

    def spatial_fn(r0):
        return [proj[r0:r0 + ROWS, SPATIAL_OUT * GROUP + h * LANES:SPATIAL_OUT * GROUP + (h + 1) * LANES]
                for h in range(HALVES)]

    def first_block_cnt():
        return j * tile + lax.broadcasted_iota(jnp.int32, (ROWS, LANES), 0) + 1

    _layer_body(x_ref.at[0], pre_g, w_in, ca_w, ca_b, lna_g, lna_b, cb_w, lnc_g, lnc_b, pl_w, pl_s,
                w_out, post_g, y_ref.at[0], ext_a, ext_b, ext_d, proj, vn_ref, mix,
                n_rows=tile, stride=1, spatial_prepare=spatial_prepare, spatial_fn=spatial_fn,
                first_block_cnt=first_block_cnt)

    @pl.when(j == pl.num_programs(1) - 1)
    def _():
        for h in range(HALVES):
            end_a, end_b, end_d = PAST_A + tile, PAST_B + tile, PAST_D + tile
            ca_out[0, :, _half(h)] = ext_a[h, end_a - (CONV_A_WIDTH - 1):end_a, :]
            cb_out[0, :, _half(h)] = ext_b[h, end_b - (CONV_B_WIDTH - 1):end_b, :]
            pd_out[0, :, _half(h)] = ext_d[h, end_d - POOL_BUF:end_d, :]


def _sample_kernel(x_ref, sa_ref, sb_ref, sd_ref, pre_g, w_in, ca_w, ca_b, lna_g, lna_b, cb_w, lnc_g,
                   lnc_b, sp_w, sp_b, pl_w, pl_s, w_out, post_g,
                   y_ref, ca_out, cb_out, pd_out, cv_out,
                   ext_a, ext_b, ext_d, proj, vn_ref, mix, xs_ref, ys_ref, *, n_steps):
    sb = SAMPLE_BLOCK
    n_rows = n_steps * sb
    for t in range(n_steps):
        xs_ref[t * sb:(t + 1) * sb, :] = x_ref[t]
    for h in range(HALVES):
        for t in range(CONV_A_WIDTH - 1):
            r = (PAST_A - (CONV_A_WIDTH - 1) + t) * sb
            ext_a[h, r:r + sb, :] = sa_ref[t, :, _half(h)]
        for t in range(CONV_B_WIDTH - 1):
            r = (PAST_B - (CONV_B_WIDTH - 1) + t) * sb
            ext_b[h, r:r + sb, :] = sb_ref[t, :, _half(h)]
        for t in range(POOL_BUF):
            r = (PAST_D - POOL_BUF + t) * sb
            ext_d[h, r:r + sb, :] = sd_ref[t, :, _half(h)]

    def spatial_fn(r0):
        t = r0 // sb
        out = []
        for h in range(HALVES):
            acc = jnp.broadcast_to(sp_b[t:t + 1, _half(h)], (sb, LANES))
            for s in range(t + 1):
                w_row = t * n_steps + s
                acc = acc + sp_w[w_row:w_row + 1, _half(h)] * vn_ref[s * sb:(s + 1) * sb, _half(h)]
            out.append(acc)
        return out

    _layer_body(xs_ref, pre_g, w_in, ca_w, ca_b, lna_g, lna_b, cb_w, lnc_g, lnc_b, pl_w, pl_s,
                w_out, post_g, ys_ref, ext_a, ext_b, ext_d, proj, vn_ref, mix,
                n_rows=n_rows, stride=sb, spatial_prepare=lambda: None, spatial_fn=spatial_fn,
                first_block_cnt=lambda: None)

    for t in range(n_steps):
        y_ref[t] = ys_ref[t * sb:(t + 1) * sb, :]
        cv_out[t] = vn_ref[t * sb:(t + 1) * sb, :]
    for h in range(HALVES):
        for t in range(CONV_A_WIDTH - 1):
            r = (PAST_A + n_steps - (CONV_A_WIDTH - 1) + t) * sb
            ca_out[t, :, _half(h)] = ext_a[h, r:r + sb, :]
        for t in range(CONV_B_WIDTH - 1):
            r = (PAST_B + n_steps - (CONV_B_WIDTH - 1) + t) * sb
            cb_out[t, :, _half(h)] = ext_b[h, r:r + sb, :]
        for t in range(POOL_BUF):
            r = (PAST_D + n_steps - POOL_BUF + t) * sb
            pd_out[t, :, _half(h)] = ext_d[h, r:r + sb, :]


def _full(shape):
    return pl.BlockSpec(shape, lambda *_: (0,) * len(shape))


def _scratch(n_rows, stride):
    return [
        pltpu.VMEM((HALVES, PAST_A * stride + n_rows, LANES), jnp.float32),
        pltpu.VMEM((HALVES, PAST_B * stride + n_rows, LANES), jnp.float32),
        pltpu.VMEM((HALVES, PAST_D * stride + n_rows, LANES), jnp.float32),
        pltpu.VMEM((n_rows, N_IN), jnp.float32),
        pltpu.VMEM((n_rows, GROUP), jnp.float32),
        pltpu.VMEM((n_rows, D_MODEL), jnp.bfloat16),
    ]


def _prompt_layer(x, p):
    batch, seq, _ = x.shape
    tile = PROMPT_TILE
    assert seq % tile == 0 and tile % CHUNK == 0 and tile >= PAST_A
    weights = (p["pre_g"], p["w_in"], p["ca_w"], p["ca_b"], p["lna_g"], p["lna_b"], p["cb_w"], p["lnc_g"],
               p["lnc_b"], p["sp_w_prompt"], p["sp_b_prompt"], p["pl_w"], p["pl_s"], p["w_out"], p["post_g"])
    tok = pl.BlockSpec((1, tile, D_MODEL), lambda b, j: (b, j, 0))

    def state(n):
        return pl.BlockSpec((1, n, GROUP), lambda b, j: (b, 0, 0))

    return pl.pallas_call(
        _prompt_kernel,
        grid=(batch, seq // tile),
        in_specs=[tok] + [_full(w.shape) for w in weights],
        out_specs=[tok, state(CONV_A_WIDTH - 1), state(CONV_B_WIDTH - 1), state(POOL_BUF)],
        out_shape=[
            jax.ShapeDtypeStruct(x.shape, x.dtype),
            jax.ShapeDtypeStruct((batch, CONV_A_WIDTH - 1, GROUP), x.dtype),
            jax.ShapeDtypeStruct((batch, CONV_B_WIDTH - 1, GROUP), x.dtype),
            jax.ShapeDtypeStruct((batch, POOL_BUF, GROUP), x.dtype),
        ],
        scratch_shapes=_scratch(tile, 1),
        compiler_params=pltpu.CompilerParams(
            dimension_semantics=("arbitrary", "arbitrary"), vmem_limit_bytes=VMEM_LIMIT),
        name="prompt_layer",
    )(x, *weights)


def _sample_layer(x_t, sa_t, sb_t, sd_t, p):
    n_steps, batch, _ = x_t.shape
    sb = SAMPLE_BLOCK
    assert batch % sb == 0 and sb == ROWS
    weights = (p["pre_g"], p["w_in"], p["ca_w"], p["ca_b"], p["lna_g"], p["lna_b"], p["cb_w"], p["lnc_g"],
               p["lnc_b"], p["sp_w_sample"], p["sp_b_sample"], p["pl_w"], p["pl_s"], p["w_out"], p["post_g"])

    def tm(n, width):
        return pl.BlockSpec((n, sb, width), lambda i: (0, i, 0))

    n_rows = n_steps * sb
    return pl.pallas_call(
        functools.partial(_sample_kernel, n_steps=n_steps),
        grid=(batch // sb,),
        in_specs=[tm(n_steps, D_MODEL), tm(CONV_A_WIDTH - 1, GROUP), tm(CONV_B_WIDTH - 1, GROUP),
                  tm(POOL_BUF, GROUP)] + [_full(w.shape) for w in weights],
        out_specs=[tm(n_steps, D_MODEL), tm(CONV_A_WIDTH - 1, GROUP), tm(CONV_B_WIDTH - 1, GROUP),
                   tm(POOL_BUF, GROUP), tm(n_steps, GROUP)],
        out_shape=[
            jax.ShapeDtypeStruct(x_t.shape, x_t.dtype),
            jax.ShapeDtypeStruct(sa_t.shape, x_t.dtype),
            jax.ShapeDtypeStruct(sb_t.shape, x_t.dtype),
            jax.ShapeDtypeStruct(sd_t.shape, x_t.dtype),
            jax.ShapeDtypeStruct((n_steps, batch, GROUP), x_t.dtype),
        ],
        scratch_shapes=_scratch(n_rows, sb) + [
            pltpu.VMEM((n_rows, D_MODEL), jnp.float32),
            pltpu.VMEM((n_rows, D_MODEL), jnp.float32),
        ],
        compiler_params=pltpu.CompilerParams(
            dimension_semantics=("arbitrary",), vmem_limit_bytes=VMEM_LIMIT),
        name="sample_layer",
    )(x_t, sa_t, sb_t, sd_t, *weights)


def _layer_params(l, n_steps, pre_norm_g, w_in, conv_a_w, conv_a_b, ln_a_g, ln_a_b, conv_b_w, ln_c_g, ln_c_b,
                  spatial_w, spatial_b, pool_w, pool_scale, w_out, post_norm_g):
    row = lambda v: v[l][None, :]
    tril = jnp.tril(jnp.ones((CHUNK, CHUNK), spatial_w.dtype))
    sw = spatial_w[l] * tril
    pool_bd = jnp.zeros((GROUP, GROUP), jnp.float32)
    for g in range(N_HEADS):
        pool_bd = lax.dynamic_update_slice(pool_bd, pool_w[l, g], (g * HEAD_DIM, g * HEAD_DIM))
    return {
        "pre_g": row(pre_norm_g), "w_in": w_in[l].astype(jnp.bfloat16),
        "ca_w": conv_a_w[l], "ca_b": row(conv_a_b), "lna_g": row(ln_a_g), "lna_b": row(ln_a_b),
        "cb_w": conv_b_w[l], "lnc_g": row(ln_c_g), "lnc_b": row(ln_c_b),
        "sp_w_prompt": sw.astype(jnp.bfloat16),
        "sp_b_prompt": jnp.repeat(spatial_b[l].T, HEAD_DIM, axis=1),
        "sp_w_sample": jnp.repeat(
            sw[:, :n_steps, :n_steps].transpose(1, 2, 0).reshape(n_steps * n_steps, N_HEADS), HEAD_DIM, axis=1),
        "sp_b_sample": jnp.repeat(spatial_b[l][:, :n_steps].T, HEAD_DIM, axis=1),
        "pl_w": pool_bd.astype(jnp.bfloat16), "pl_s": row(pool_scale),
        "w_out": w_out[l].astype(jnp.bfloat16), "post_g": row(post_norm_g),
    }


def kernel(x_prompt, x_sample, state_conv_a, state_conv_b, state_pool, pre_norm_g, w_in, conv_a_w, conv_a_b, ln_a_g, ln_a_b, conv_b_w, ln_c_g, ln_c_b, spatial_w, spatial_b, pool_w, pool_scale, w_out, post_norm_g):
    depth = w_in.shape[0]
    n_steps = x_sample.shape[1]
    to_tm = lambda v: jnp.swapaxes(v, -3, -2)
    yp, ys = x_prompt, to_tm(x_sample)
    sa, sb, sd = to_tm(state_conv_a), to_tm(state_conv_b), to_tm(state_pool)
    outs = [[] for _ in range(7)]
    for l in range(depth):
        p = _layer_params(l, n_steps, pre_norm_g, w_in, conv_a_w, conv_a_b, ln_a_g, ln_a_b, conv_b_w, ln_c_g,
                          ln_c_b, spatial_w, spatial_b, pool_w, pool_scale, w_out, post_norm_g)
        yp, a_p, b_p, d_p = _prompt_layer(yp, p)
        ys, a_s, b_s, d_s, v_s = _sample_layer(ys, sa[l], sb[l], sd[l], p)
        for o, v in zip(outs, (a_p, to_tm(a_s), b_p, to_tm(b_s), d_p, to_tm(d_s), to_tm(v_s))):
            o.append(v)
    return (yp, to_tm(ys)) + tuple(jnp.stack(o) for o in outs)
```

```python
import functools
import types

import jax
import jax.numpy as jnp
from jax import lax
from jax.experimental import pallas as pl
from jax.experimental.pallas import tpu as pltpu

D_MODEL = 1024
GROUP = 256
HEAD_DIM = 64
N_HEADS = 4
CONV_A_WIDTH = 31
CONV_B_WIDTH = 3
CHUNK = 128
POOL_BUF = 15
EPS = 1e-6
LANES = 128
BF16_ROWS = 16
HALVES = GROUP // LANES

(A_VAL, A_GATE, Z_A, B_B, B_C, B_X, Z_B, C_U, C_V, Z_C, D_X, Z_D) = range(12)
N_IN = 12 * GROUP
SPATIAL_OUT = C_V
POOL_OUT = D_X

PAST_A = 32
PAST_B = 8
PAST_D = 16

TILES = 2
PROMPT_TILE = 256
SAMPLE_BLOCK = 32
ROWS = 32

VMEM_LIMIT = 48 * 1024 * 1024


def _sigmoid(v):
    return 1.0 / (1.0 + jnp.exp(-v))


def _silu(v):
    return v * _sigmoid(v)


def _rowsum(vs):
    tot = vs[0]
    for v in vs[1:]:
        tot = tot + v
    return jnp.sum(tot, axis=-1, keepdims=True)


def _half(h):
    return slice(h * LANES, (h + 1) * LANES)


def _slot(slot, h):
    return slice(slot * GROUP + h * LANES, slot * GROUP + (h + 1) * LANES)


def _layernorm2(v0, v1, g_ref, b_ref):
    mu = _rowsum([v0, v1]) * (1.0 / GROUP)
    d0, d1 = v0 - mu, v1 - mu
    var = _rowsum([d0 * d0, d1 * d1]) * (1.0 / GROUP)
    inv = lax.rsqrt(var + EPS)
    return (d0 * inv * g_ref[:, _half(0)] + b_ref[:, _half(0)],
            d1 * inv * g_ref[:, _half(1)] + b_ref[:, _half(1)])


def _copy_rows(dst, src, n_rows, chunk):
    def body(i, carry):
        r = pl.multiple_of(i * chunk, chunk)
        dst[pl.ds(r, chunk), :] = src[pl.ds(r, chunk), :].astype(dst.dtype)
        return carry
    lax.fori_loop(0, n_rows // chunk, body, 0)


def _phase_in(t, p):
    for r0 in range(0, t.n_rows, ROWS):
        xb = t.x[r0:r0 + ROWS, :]
        ms = jnp.sum(xb * xb, axis=-1, keepdims=True) * (1.0 / D_MODEL)
        hb = xb * lax.rsqrt(ms + EPS) * p.pre_g[...]
        t.mix[r0:r0 + ROWS, :] = hb.astype(jnp.bfloat16)
    t.proj[...] = jnp.dot(t.mix[...], p.w_in[...], preferred_element_type=jnp.float32)


def _phase_mix(t, p):
    n_rows, stride = t.n_rows, t.stride
    cur_a, cur_b, cur_d = PAST_A * stride, PAST_B * stride, PAST_D * stride
    lo = lax.broadcasted_iota(jnp.int32, (ROWS, LANES), 1) < HEAD_DIM
    inv_w = [jnp.where(lo, 1.0 / 2, 1.0 / 4), jnp.where(lo, 1.0 / 8, 1.0 / 16)]
    win = [jnp.where(lo, 2, 4), jnp.where(lo, 8, 16)]

    def pj(r0, slot, h):
        return t.proj[r0:r0 + ROWS, _slot(slot, h)]

    for r0 in range(0, n_rows, ROWS):
        for h in range(HALVES):
            t.ext_a[h, cur_a + r0:cur_a + r0 + ROWS, :] = pj(r0, A_VAL, h) * _sigmoid(pj(r0, A_GATE, h))
            t.ext_b[h, cur_b + r0:cur_b + r0 + ROWS, :] = pj(r0, B_C, h) * pj(r0, B_X, h)
            t.ext_d[h, cur_d + r0:cur_d + r0 + ROWS, :] = pj(r0, D_X, h)
        v0, v1 = _layernorm2(pj(r0, C_V, 0), pj(r0, C_V, 1), p.lnc_g, p.lnc_b)
        t.vn[r0:r0 + ROWS, _half(0)] = v0
        t.vn[r0:r0 + ROWS, _half(1)] = v1

        def tap(h, j, extra=0):
            off = cur_d - j * stride + r0 - extra
            return t.ext_d[h, off:off + ROWS + extra, :]

        a2 = tap(0, 0) + tap(0, 1)
        a4 = a2 + (tap(0, 2) + tap(0, 3))
        sums = [jnp.where(lo, a2, a4)]
        if stride == 1:
            s8 = tap(1, 0, 8)
            for j in range(1, 8):
                s8 = s8 + tap(1, j, 8)
            s8_cur, s8_prev = s8[8:], s8[:ROWS]
        else:
            s8_cur, s8_prev = tap(1, 0), tap(1, 8)
            for j in range(1, 8):
                s8_cur = s8_cur + tap(1, j)
                s8_prev = s8_prev + tap(1, 8 + j)
        sums.append(jnp.where(lo, s8_cur, s8_cur + s8_prev))
        cnt = t.first_block_cnt() if r0 == 0 else None
        for h in range(HALVES):
            if cnt is None:
                mean = sums[h] * inv_w[h]
            else:
                mean = sums[h] / jnp.minimum(cnt, win[h]).astype(jnp.float32)
            t.mix[r0:r0 + ROWS, _slot(3, h)] = (mean - tap(h, 0)).astype(jnp.bfloat16)

    t.spatial_prepare()
    t.proj[:, POOL_OUT * GROUP:(POOL_OUT + 1) * GROUP] = jnp.dot(
        t.mix[:, 3 * GROUP:4 * GROUP], p.pl_w[...], preferred_element_type=jnp.float32)

    for r0 in range(0, n_rows, ROWS):
        ya = []
        for h in range(HALVES):
            acc = jnp.broadcast_to(p.ca_b[:, _half(h)], (ROWS, LANES))
            for k in range(CONV_A_WIDTH):
                off = cur_a - (CONV_A_WIDTH - 1 - k) * stride + r0
                acc = acc + p.ca_w[k:k + 1, _half(h)] * t.ext_a[h, off:off + ROWS, :]
            ya.append(acc)
        na = _layernorm2(ya[0], ya[1], p.lna_g, p.lna_b)
        for h in range(HALVES):
            t.mix[r0:r0 + ROWS, _slot(0, h)] = (_silu(na[h]) * _silu(pj(r0, Z_A, h))).astype(jnp.bfloat16)

        for h in range(HALVES):
            acc = None
            for k in range(CONV_B_WIDTH):
                off = cur_b - (CONV_B_WIDTH - 1 - k) * stride + r0
                term = p.cb_w[k:k + 1, _half(h)] * t.ext_b[h, off:off + ROWS, :]
                acc = term if acc is None else acc + term
            yb = pj(r0, B_B, h) * acc * _silu(pj(r0, Z_B, h))
            t.mix[r0:r0 + ROWS, _slot(1, h)] = yb.astype(jnp.bfloat16)

        s = t.spatial_fn(r0)
        for h in range(HALVES):
            yc = pj(r0, C_U, h) * s[h] * _silu(pj(r0, Z_C, h))
            t.mix[r0:r0 + ROWS, _slot(2, h)] = yc.astype(jnp.bfloat16)

        for h in range(HALVES):
            yd = pj(r0, POOL_OUT, h) * p.pl_s[:, _half(h)] * _silu(pj(r0, Z_D, h))
            t.mix[r0:r0 + ROWS, _slot(3, h)] = yd.astype(jnp.bfloat16)

    t.proj[:, 0:D_MODEL] = jnp.dot(t.mix[...], p.w_out[...], preferred_element_type=jnp.float32)


def _phase_out(t, p):
    for r0 in range(0, t.n_rows, ROWS):
        ob = t.proj[r0:r0 + ROWS, 0:D_MODEL]
        ms = jnp.sum(ob * ob, axis=-1, keepdims=True) * (1.0 / D_MODEL)
        t.y[r0:r0 + ROWS, :] = t.x[r0:r0 + ROWS, :] + ob * lax.rsqrt(ms + EPS) * p.post_g[...]


def _run_tiles(tiles, p):
    for phase in (_phase_in, _phase_mix, _phase_out):
        for t in tiles:
            phase(t, p)


def _params(pre_g, w_in, ca_w, ca_b, lna_g, lna_b, cb_w, lnc_g, lnc_b, pl_w, pl_s, w_out, post_g):
    return types.SimpleNamespace(
        pre_g=pre_g.at[0], w_in=w_in, ca_w=ca_w.at[0], ca_b=ca_b.at[0], lna_g=lna_g.at[0], lna_b=lna_b.at[0],
        cb_w=cb_w.at[0], lnc_g=lnc_g.at[0], lnc_b=lnc_b.at[0], pl_w=pl_w.at[0], pl_s=pl_s.at[0],
        w_out=w_out, post_g=post_g.at[0])


def _prompt_kernel(x_ref, pre_g, w_in_blk, ca_w, ca_b, lna_g, lna_b, cb_w, lnc_g, lnc_b, sp_w_raw, sp_b,
                   pl_w, pl_s, w_out_blk, post_g,
                   y_ref, ca_out, cb_out, pd_out,
                   w_in, w_out, sp_w, ext_a, ext_b, ext_d, proj, vn_ref, mix):
    tile = PROMPT_TILE
    j = pl.program_id(1)

    @pl.when((pl.program_id(0) == 0) & (j == 0))
    def _():
        _copy_rows(w_in, w_in_blk.at[0], D_MODEL, BF16_ROWS)
        _copy_rows(w_out, w_out_blk.at[0], D_MODEL, BF16_ROWS)
        keep = (lax.broadcasted_iota(jnp.int32, (CHUNK, CHUNK), 1)
                <= lax.broadcasted_iota(jnp.int32, (CHUNK, CHUNK), 0))
        for hd in range(N_HEADS):
            sp_w[hd] = jnp.where(keep, sp_w_raw[0, hd], 0.0).astype(jnp.bfloat16)

    @pl.when(j == 0)
    def _():
        ext_a[:, :, 0:PAST_A, :] = jnp.zeros((TILES, HALVES, PAST_A, LANES), jnp.float32)
        ext_b[:, :, 0:PAST_B, :] = jnp.zeros((TILES, HALVES, PAST_B, LANES), jnp.float32)
        ext_d[:, :, 0:PAST_D, :] = jnp.zeros((TILES, HALVES, PAST_D, LANES), jnp.float32)

    @pl.when(j > 0)
    def _():
        ext_a[:, :, 0:PAST_A, :] = ext_a[:, :, tile:tile + PAST_A, :]
        ext_b[:, :, 0:PAST_B, :] = ext_b[:, :, tile:tile + PAST_B, :]
        ext_d[:, :, 0:PAST_D, :] = ext_d[:, :, tile:tile + PAST_D, :]

    lo = lax.broadcasted_iota(jnp.int32, (CHUNK, LANES), 1) < HEAD_DIM
    p = _params(pre_g, w_in, ca_w, ca_b, lna_g, lna_b, cb_w, lnc_g, lnc_b, pl_w, pl_s, w_out, post_g)

    def make_tile(i):
        t_proj, t_vn = proj.at[i], vn_ref.at[i]

        def spatial_prepare():
            for c0 in range(0, tile, CHUNK):
                for h in range(HALVES):
                    vb = t_vn[c0:c0 + CHUNK, _half(h)].astype(jnp.bfloat16)
                    s_even = jnp.dot(sp_w[2 * h], vb, preferred_element_type=jnp.float32)
                    s_odd = jnp.dot(sp_w[2 * h + 1], vb, preferred_element_type=jnp.float32)
                    t_proj[c0:c0 + CHUNK, _slot(SPATIAL_OUT, h)] = (
                        jnp.where(lo, s_even, s_odd) + sp_b[0, :, _half(h)])

        def spatial_fn(r0):
            return [t_proj[r0:r0 + ROWS, _slot(SPATIAL_OUT, h)] for h in range(HALVES)]

        def first_block_cnt():
            return j * tile + lax.broadcasted_iota(jnp.int32, (ROWS, LANES), 0) + 1

        return types.SimpleNamespace(
            x=x_ref.at[i], y=y_ref.at[i], ext_a=ext_a.at[i], ext_b=ext_b.at[i], ext_d=ext_d.at[i],
            proj=t_proj, vn=t_vn, mix=mix.at[i], n_rows=tile, stride=1,
            spatial_prepare=spatial_prepare, spatial_fn=spatial_fn, first_block_cnt=first_block_cnt)

    _run_tiles([make_tile(i) for i in range(TILES)], p)

    @pl.when(j == pl.num_programs(1) - 1)
    def _():
        end_a, end_b, end_d = PAST_A + tile, PAST_B + tile, PAST_D + tile
        for i in range(TILES):
            for h in range(HALVES):
                ca_out[i, :, _half(h)] = ext_a[i, h, end_a - (CONV_A_WIDTH - 1):end_a, :]
                cb_out[i, :, _half(h)] = ext_b[i, h, end_b - (CONV_B_WIDTH - 1):end_b, :]
                pd_out[i, :, _half(h)] = ext_d[i, h, end_d - POOL_BUF:end_d, :]


def _sample_kernel(x_ref, sa_ref, sb_ref, sd_ref, pre_g, w_in_blk, ca_w, ca_b, lna_g, lna_b, cb_w, lnc_g,
                   lnc_b, sp_w, sp_b, pl_w, pl_s, w_out_blk, post_g,
                   y_ref, ca_out, cb_out, pd_out, cv_out,
                   w_in, w_out, ext_a, ext_b, ext_d, proj, vn_ref, mix, xs_ref, ys_ref, *, n_steps):
    sb = SAMPLE_BLOCK
    n_rows = n_steps * sb

    @pl.when(pl.program_id(0) == 0)
    def _():
        _copy_rows(w_in, w_in_blk.at[0], D_MODEL, BF16_ROWS)
        _copy_rows(w_out, w_out_blk.at[0], D_MODEL, BF16_ROWS)

    for i in range(TILES):
        seqs = slice(i * sb, (i + 1) * sb)
        for s in range(n_steps):
            xs_ref[i, s * sb:(s + 1) * sb, :] = x_ref[s, seqs, :]
        for h in range(HALVES):
            for s in range(CONV_A_WIDTH - 1):
                r = (PAST_A - (CONV_A_WIDTH - 1) + s) * sb
                ext_a[i, h, r:r + sb, :] = sa_ref[0, s, seqs, _half(h)]
            for s in range(CONV_B_WIDTH - 1):
                r = (PAST_B - (CONV_B_WIDTH - 1) + s) * sb
                ext_b[i, h, r:r + sb, :] = sb_ref[0, s, seqs, _half(h)]
            for s in range(POOL_BUF):
                r = (PAST_D - POOL_BUF + s) * sb
                ext_d[i, h, r:r + sb, :] = sd_ref[0, s, seqs, _half(h)]

    p = _params(pre_g, w_in, ca_w, ca_b, lna_g, lna_b, cb_w, lnc_g, lnc_b, pl_w, pl_s, w_out, post_g)

    def make_tile(i):
        t_vn = vn_ref.at[i]

        def spatial_fn(r0):
            s = r0 // sb
            out = []
            for h in range(HALVES):
                acc = jnp.broadcast_to(sp_b[0, s:s + 1, _half(h)], (sb, LANES))
                for s2 in range(s + 1):
                    w_row = s * n_steps + s2
                    acc = acc + sp_w[0, w_row:w_row + 1, _half(h)] * t_vn[s2 * sb:(s2 + 1) * sb, _half(h)]
                out.append(acc)
            return out

        return types.SimpleNamespace(
            x=xs_ref.at[i], y=ys_ref.at[i], ext_a=ext_a.at[i], ext_b=ext_b.at[i], ext_d=ext_d.at[i],
            proj=proj.at[i], vn=t_vn, mix=mix.at[i], n_rows=n_rows, stride=sb,
            spatial_prepare=lambda: None, spatial_fn=spatial_fn, first_block_cnt=lambda: None)

    _run_tiles([make_tile(i) for i in range(TILES)], p)

    for i in range(TILES):
        seqs = slice(i * sb, (i + 1) * sb)
        for s in range(n_steps):
            y_ref[s, seqs, :] = ys_ref[i, s * sb:(s + 1) * sb, :]
            cv_out[s, seqs, :] = vn_ref[i, s * sb:(s + 1) * sb, :]
        for h in range(HALVES):
            for s in range(CONV_A_WIDTH - 1):
                r = (PAST_A + n_steps - (CONV_A_WIDTH - 1) + s) * sb
                ca_out[s, seqs, _half(h)] = ext_a[i, h, r:r + sb, :]
            for s in range(CONV_B_WIDTH - 1):
                r = (PAST_B + n_steps - (CONV_B_WIDTH - 1) + s) * sb
                cb_out[s, seqs, _half(h)] = ext_b[i, h, r:r + sb, :]
            for s in range(POOL_BUF):
                r = (PAST_D + n_steps - POOL_BUF + s) * sb
                pd_out[s, seqs, _half(h)] = ext_d[i, h, r:r + sb, :]


def _layer_spec(arr, l, single_buffer=False):
    shape = (1,) + arr.shape[1:]
    index_map = lambda *_: (l,) + (0,) * (len(shape) - 1)
    if single_buffer:
        return pl.BlockSpec(shape, index_map, pipeline_mode=pl.Buffered(1))
    return pl.BlockSpec(shape, index_map)


def _scratch(n_rows, stride):
    return [
        pltpu.VMEM((D_MODEL, N_IN), jnp.bfloat16),
        pltpu.VMEM((D_MODEL, D_MODEL), jnp.bfloat16),
    ], [
        pltpu.VMEM((TILES, HALVES, PAST_A * stride + n_rows, LANES), jnp.float32),
        pltpu.VMEM((TILES, HALVES, PAST_B * stride + n_rows, LANES), jnp.float32),
        pltpu.VMEM((TILES, HALVES, PAST_D * stride + n_rows, LANES), jnp.float32),
        pltpu.VMEM((TILES, n_rows, N_IN), jnp.float32),
        pltpu.VMEM((TILES, n_rows, GROUP), jnp.float32),
        pltpu.VMEM((TILES, n_rows, D_MODEL), jnp.bfloat16),
    ]


def _weight_specs(w, l, names):
    return [_layer_spec(w[n], l, single_buffer=n in ("w_in", "w_out")) for n in names]


_PROMPT_WEIGHTS = ("pre_g", "w_in", "ca_w", "ca_b", "lna_g", "lna_b", "cb_w", "lnc_g", "lnc_b", "sp_w",
                   "sp_b_prompt", "pl_w", "pl_s", "w_out", "post_g")
_SAMPLE_WEIGHTS = ("pre_g", "w_in", "ca_w", "ca_b", "lna_g", "lna_b", "cb_w", "lnc_g", "lnc_b", "sp_w_sample",
                   "sp_b_sample", "pl_w", "pl_s", "w_out", "post_g")


def _prompt_layer(x, w, l):
    batch, seq, _ = x.shape
    tile = PROMPT_TILE
    assert seq % tile == 0 and tile % CHUNK == 0 and tile >= PAST_A and batch % TILES == 0
    tok = pl.BlockSpec((TILES, tile, D_MODEL), lambda b, j: (b, j, 0))

    def state(n):
        return pl.BlockSpec((TILES, n, GROUP), lambda b, j: (b, 0, 0))

    weight_scratch, tile_scratch = _scratch(tile, 1)
    return pl.pallas_call(
        _prompt_kernel,
        grid=(batch // TILES, seq // tile),
        in_specs=[tok] + _weight_specs(w, l, _PROMPT_WEIGHTS),
        out_specs=[tok, state(CONV_A_WIDTH - 1), state(CONV_B_WIDTH - 1), state(POOL_BUF)],
        out_shape=[
            jax.ShapeDtypeStruct(x.shape, x.dtype),
            jax.ShapeDtypeStruct((batch, CONV_A_WIDTH - 1, GROUP), x.dtype),
            jax.ShapeDtypeStruct((batch, CONV_B_WIDTH - 1, GROUP), x.dtype),
            jax.ShapeDtypeStruct((batch, POOL_BUF, GROUP), x.dtype),
        ],
        scratch_shapes=weight_scratch + [pltpu.VMEM((N_HEADS, CHUNK, CHUNK), jnp.bfloat16)] + tile_scratch,
        compiler_params=pltpu.CompilerParams(
            dimension_semantics=("arbitrary", "arbitrary"), vmem_limit_bytes=VMEM_LIMIT),
        name="prompt_layer",
    )(x, *[w[n] for n in _PROMPT_WEIGHTS])


def _sample_layer(x_t, sa_t, sb_t, sd_t, w, l):
    n_steps, batch, _ = x_t.shape
    sb = SAMPLE_BLOCK
    blk = TILES * sb
    assert batch % blk == 0 and sb == ROWS

    def tm(n, width):
        return pl.BlockSpec((n, blk, width), lambda i: (0, i, 0))

    def tm_l(n, width):
        return pl.BlockSpec((1, n, blk, width), lambda i: (l, 0, i, 0))

    n_rows = n_steps * sb
    weight_scratch, tile_scratch = _scratch(n_rows, sb)
    return pl.pallas_call(
        functools.partial(_sample_kernel, n_steps=n_steps),
        grid=(batch // blk,),
        in_specs=[tm(n_steps, D_MODEL), tm_l(CONV_A_WIDTH - 1, GROUP), tm_l(CONV_B_WIDTH - 1, GROUP),
                  tm_l(POOL_BUF, GROUP)] + _weight_specs(w, l, _SAMPLE_WEIGHTS),
        out_specs=[tm(n_steps, D_MODEL), tm(CONV_A_WIDTH - 1, GROUP), tm(CONV_B_WIDTH - 1, GROUP),
                   tm(POOL_BUF, GROUP), tm(n_steps, GROUP)],
        out_shape=[
            jax.ShapeDtypeStruct(x_t.shape, x_t.dtype),
            jax.ShapeDtypeStruct(sa_t.shape[1:], x_t.dtype),
            jax.ShapeDtypeStruct(sb_t.shape[1:], x_t.dtype),
            jax.ShapeDtypeStruct(sd_t.shape[1:], x_t.dtype),
            jax.ShapeDtypeStruct((n_steps, batch, GROUP), x_t.dtype),
        ],
        scratch_shapes=weight_scratch + tile_scratch + [
            pltpu.VMEM((TILES, n_rows, D_MODEL), jnp.float32),
            pltpu.VMEM((TILES, n_rows, D_MODEL), jnp.float32),
        ],
        compiler_params=pltpu.CompilerParams(
            dimension_semantics=("arbitrary",), vmem_limit_bytes=VMEM_LIMIT),
        name="sample_layer",
    )(x_t, sa_t, sb_t, sd_t, *[w[n] for n in _SAMPLE_WEIGHTS])


def _prepare_weights(n_steps, pre_norm_g, w_in, conv_a_w, conv_a_b, ln_a_g, ln_a_b, conv_b_w, ln_c_g, ln_c_b,
                     spatial_w, spatial_b, pool_w, pool_scale, w_out, post_norm_g):
    row = lambda v: v[:, None, :]
    depth = w_in.shape[0]
    eye = jnp.eye(N_HEADS, dtype=pool_w.dtype)
    pool_bd = jnp.einsum("lgcd,gh->lgchd", pool_w, eye).reshape(depth, GROUP, GROUP)
    sw = spatial_w[:, :, :n_steps, :n_steps] * jnp.tril(jnp.ones((n_steps, n_steps), spatial_w.dtype))
    sw = sw.transpose(0, 2, 3, 1).reshape(depth, n_steps * n_steps, N_HEADS)
    return {
        "pre_g": row(pre_norm_g), "w_in": w_in.astype(jnp.bfloat16),
        "ca_w": conv_a_w, "ca_b": row(conv_a_b), "lna_g": row(ln_a_g), "lna_b": row(ln_a_b),
        "cb_w": conv_b_w, "lnc_g": row(ln_c_g), "lnc_b": row(ln_c_b),
        "sp_w": spatial_w,
        "sp_b_prompt": jnp.repeat(spatial_b.transpose(0, 2, 1), HEAD_DIM, axis=2),
        "sp_w_sample": jnp.repeat(sw, HEAD_DIM, axis=2),
        "sp_b_sample": jnp.repeat(spatial_b[:, :, :n_steps].transpose(0, 2, 1), HEAD_DIM, axis=2),
        "pl_w": pool_bd.astype(jnp.bfloat16), "pl_s": row(pool_scale),
        "w_out": w_out.astype(jnp.bfloat16), "post_g": row(post_norm_g),
    }


def kernel(x_prompt, x_sample, state_conv_a, state_conv_b, state_pool, pre_norm_g, w_in, conv_a_w, conv_a_b, ln_a_g, ln_a_b, conv_b_w, ln_c_g, ln_c_b, spatial_w, spatial_b, pool_w, pool_scale, w_out, post_norm_g):
    depth = w_in.shape[0]
    n_steps = x_sample.shape[1]
    to_tm = lambda v: jnp.swapaxes(v, -3, -2)
    w = _prepare_weights(n_steps, pre_norm_g, w_in, conv_a_w, conv_a_b, ln_a_g, ln_a_b, conv_b_w, ln_c_g,
                         ln_c_b, spatial_w, spatial_b, pool_w, pool_scale, w_out, post_norm_g)
    yp, ys = x_prompt, to_tm(x_sample)
    sa, sb, sd = to_tm(state_conv_a), to_tm(state_conv_b), to_tm(state_pool)
    outs = [[] for _ in range(7)]
    for l in range(depth):
        yp, a_p, b_p, d_p = _prompt_layer(yp, w, l)
        ys, a_s, b_s, d_s, v_s = _sample_layer(ys, sa, sb, sd, w, l)
        for o, v in zip(outs, (a_p, a_s, b_p, b_s, d_p, d_s, v_s)):
            o.append(v)
    outs = [jnp.stack(o) for o in outs]
    for k in (1, 3, 5, 6):
        outs[k] = to_tm(outs[k])
    return (yp, to_tm(ys)) + tuple(outs)
```

```python
import functools
import types

import jax
import jax.numpy as jnp
from jax import lax
from jax.experimental import pallas as pl
from jax.experimental.pallas import tpu as pltpu

D_MODEL = 1024
GROUP = 256
HEAD_DIM = 64
N_HEADS = 4
CONV_A_WIDTH = 31
CONV_B_WIDTH = 3
CHUNK = 128
POOL_BUF = 15
EPS = 1e-6
LANES = 128
BF16_ROWS = 16
HALVES = GROUP // LANES

(A_VAL, A_GATE, Z_A, B_B, B_C, B_X, Z_B, C_U, C_V, Z_C, D_X, Z_D) = range(12)
N_IN = 12 * GROUP
SPATIAL_OUT = C_V
POOL_OUT = D_X

PAST_A = 32
PAST_B = 8
PAST_D = 16

PROMPT_TILES = 1
SAMPLE_TILES = 2
PROMPT_TILE = 1024
SAMPLE_BLOCK = 32
ROWS = 32

VMEM_LIMIT = 58 * 1024 * 1024


def _sigmoid(v):
    return 1.0 / (1.0 + jnp.exp(-v))


def _silu(v):
    return v * _sigmoid(v)


def _rowsum(vs):
    tot = vs[0]
    for v in vs[1:]:
        tot = tot + v
    return jnp.sum(tot, axis=-1, keepdims=True)


def _half(h):
    return slice(h * LANES, (h + 1) * LANES)


def _slot(slot, h):
    return slice(slot * GROUP + h * LANES, slot * GROUP + (h + 1) * LANES)


def _layernorm2(v0, v1, g_ref, b_ref):
    mu = _rowsum([v0, v1]) * (1.0 / GROUP)
    d0, d1 = v0 - mu, v1 - mu
    var = _rowsum([d0 * d0, d1 * d1]) * (1.0 / GROUP)
    inv = lax.rsqrt(var + EPS)
    return (d0 * inv * g_ref[:, _half(0)] + b_ref[:, _half(0)],
            d1 * inv * g_ref[:, _half(1)] + b_ref[:, _half(1)])


def _copy_rows(dst, src, n_rows, chunk):
    def body(i, carry):
        r = pl.multiple_of(i * chunk, chunk)
        dst[pl.ds(r, chunk), :] = src[pl.ds(r, chunk), :].astype(dst.dtype)
        return carry
    lax.fori_loop(0, n_rows // chunk, body, 0)


def _phase_in(t, p):
    for r0 in range(0, t.n_rows, ROWS):
        xb = t.x[r0:r0 + ROWS, :]
        ms = jnp.sum(xb * xb, axis=-1, keepdims=True) * (1.0 / D_MODEL)
        hb = xb * lax.rsqrt(ms + EPS) * p.pre_g[...]
        t.mix[r0:r0 + ROWS, :] = hb.astype(jnp.bfloat16)
    t.proj[...] = jnp.dot(t.mix[...], p.w_in[...], preferred_element_type=jnp.float32)


def _phase_mix(t, p):
    n_rows, stride = t.n_rows, t.stride
    cur_a, cur_b, cur_d = PAST_A * stride, PAST_B * stride, PAST_D * stride
    lo = lax.broadcasted_iota(jnp.int32, (ROWS, LANES), 1) < HEAD_DIM
    inv_w = [jnp.where(lo, 1.0 / 2, 1.0 / 4), jnp.where(lo, 1.0 / 8, 1.0 / 16)]
    win = [jnp.where(lo, 2, 4), jnp.where(lo, 8, 16)]

    def pj(r0, slot, h):
        return t.proj[r0:r0 + ROWS, _slot(slot, h)]

    for r0 in range(0, n_rows, ROWS):
        for h in range(HALVES):
            t.ext_a[h, cur_a + r0:cur_a + r0 + ROWS, :] = pj(r0, A_VAL, h) * _sigmoid(pj(r0, A_GATE, h))
            t.ext_b[h, cur_b + r0:cur_b + r0 + ROWS, :] = pj(r0, B_C, h) * pj(r0, B_X, h)
            t.ext_d[h, cur_d + r0:cur_d + r0 + ROWS, :] = pj(r0, D_X, h)
        v0, v1 = _layernorm2(pj(r0, C_V, 0), pj(r0, C_V, 1), p.lnc_g, p.lnc_b)
        t.vn[r0:r0 + ROWS, _half(0)] = v0
        t.vn[r0:r0 + ROWS, _half(1)] = v1

        def tap(h, j, extra=0):
            off = cur_d - j * stride + r0 - extra
            return t.ext_d[h, off:off + ROWS + extra, :]

        a2 = tap(0, 0) + tap(0, 1)
        a4 = a2 + (tap(0, 2) + tap(0, 3))
        sums = [jnp.where(lo, a2, a4)]
        if stride == 1:
            s8 = tap(1, 0, 8)
            for j in range(1, 8):
                s8 = s8 + tap(1, j, 8)
            s8_cur, s8_prev = s8[8:], s8[:ROWS]
        else:
            s8_cur, s8_prev = tap(1, 0), tap(1, 8)
            for j in range(1, 8):
                s8_cur = s8_cur + tap(1, j)
                s8_prev = s8_prev + tap(1, 8 + j)
        sums.append(jnp.where(lo, s8_cur, s8_cur + s8_prev))
        cnt = t.first_block_cnt() if r0 == 0 else None
        for h in range(HALVES):
            if cnt is None:
                mean = sums[h] * inv_w[h]
            else:
                mean = sums[h] / jnp.minimum(cnt, win[h]).astype(jnp.float32)
            t.mix[r0:r0 + ROWS, _slot(3, h)] = (mean - tap(h, 0)).astype(jnp.bfloat16)

    t.spatial_prepare()
    t.proj[:, POOL_OUT * GROUP:(POOL_OUT + 1) * GROUP] = jnp.dot(
        t.mix[:, 3 * GROUP:4 * GROUP], p.pl_w[...], preferred_element_type=jnp.float32)

    for r0 in range(0, n_rows, ROWS):
        ya = []
        for h in range(HALVES):
            acc = jnp.broadcast_to(p.ca_b[:, _half(h)], (ROWS, LANES))
            for k in range(CONV_A_WIDTH):
                off = cur_a - (CONV_A_WIDTH - 1 - k) * stride + r0
                acc = acc + p.ca_w[k:k + 1, _half(h)] * t.ext_a[h, off:off + ROWS, :]
            ya.append(acc)
        na = _layernorm2(ya[0], ya[1], p.lna_g, p.lna_b)
        for h in range(HALVES):
            t.mix[r0:r0 + ROWS, _slot(0, h)] = (_silu(na[h]) * _silu(pj(r0, Z_A, h))).astype(jnp.bfloat16)

        for h in range(HALVES):
            acc = None
            for k in range(CONV_B_WIDTH):
                off = cur_b - (CONV_B_WIDTH - 1 - k) * stride + r0
                term = p.cb_w[k:k + 1, _half(h)] * t.ext_b[h, off:off + ROWS, :]
                acc = term if acc is None else acc + term
            yb = pj(r0, B_B, h) * acc * _silu(pj(r0, Z_B, h))
            t.mix[r0:r0 + ROWS, _slot(1, h)] = yb.astype(jnp.bfloat16)

        s = t.spatial_fn(r0)
        for h in range(HALVES):
            yc = pj(r0, C_U, h) * s[h] * _silu(pj(r0, Z_C, h))
            t.mix[r0:r0 + ROWS, _slot(2, h)] = yc.astype(jnp.bfloat16)

        for h in range(HALVES):
            yd = pj(r0, POOL_OUT, h) * p.pl_s[:, _half(h)] * _silu(pj(r0, Z_D, h))
            t.mix[r0:r0 + ROWS, _slot(3, h)] = yd.astype(jnp.bfloat16)

    t.proj[:, 0:D_MODEL] = jnp.dot(t.mix[...], p.w_out[...], preferred_element_type=jnp.float32)


def _phase_out(t, p):
    for r0 in range(0, t.n_rows, ROWS):
        ob = t.proj[r0:r0 + ROWS, 0:D_MODEL]
        ms = jnp.sum(ob * ob, axis=-1, keepdims=True) * (1.0 / D_MODEL)
        t.y[r0:r0 + ROWS, :] = t.x[r0:r0 + ROWS, :] + ob * lax.rsqrt(ms + EPS) * p.post_g[...]


def _run_tiles(tiles, p):
    for phase in (_phase_in, _phase_mix, _phase_out):
        for t in tiles:
            phase(t, p)


def _params(pre_g, w_in, ca_w, ca_b, lna_g, lna_b, cb_w, lnc_g, lnc_b, pl_w, pl_s, w_out, post_g):
    return types.SimpleNamespace(
        pre_g=pre_g.at[0], w_in=w_in, ca_w=ca_w.at[0], ca_b=ca_b.at[0], lna_g=lna_g.at[0], lna_b=lna_b.at[0],
        cb_w=cb_w.at[0], lnc_g=lnc_g.at[0], lnc_b=lnc_b.at[0], pl_w=pl_w.at[0], pl_s=pl_s.at[0],
        w_out=w_out, post_g=post_g.at[0])


def _prompt_kernel(x_ref, pre_g, w_in_blk, ca_w, ca_b, lna_g, lna_b, cb_w, lnc_g, lnc_b, sp_w_raw, sp_b,
                   pl_w, pl_s, w_out_blk, post_g,
                   y_ref, ca_out, cb_out, pd_out,
                   w_in, w_out, sp_w, ext_a, ext_b, ext_d, proj, vn_ref, mix):
    tile = PROMPT_TILE
    n_tiles = x_ref.shape[0]
    j = pl.program_id(1)

    @pl.when((pl.program_id(0) == 0) & (j == 0))
    def _():
        _copy_rows(w_in, w_in_blk.at[0], D_MODEL, BF16_ROWS)
        _copy_rows(w_out, w_out_blk.at[0], D_MODEL, BF16_ROWS)
        keep = (lax.broadcasted_iota(jnp.int32, (CHUNK, CHUNK), 1)
                <= lax.broadcasted_iota(jnp.int32, (CHUNK, CHUNK), 0))
        for hd in range(N_HEADS):
            sp_w[hd] = jnp.where(keep, sp_w_raw[0, hd], 0.0).astype(jnp.bfloat16)

    @pl.when(j == 0)
    def _():
        ext_a[:, :, 0:PAST_A, :] = jnp.zeros((n_tiles, HALVES, PAST_A, LANES), jnp.float32)
        ext_b[:, :, 0:PAST_B, :] = jnp.zeros((n_tiles, HALVES, PAST_B, LANES), jnp.float32)
        ext_d[:, :, 0:PAST_D, :] = jnp.zeros((n_tiles, HALVES, PAST_D, LANES), jnp.float32)

    @pl.when(j > 0)
    def _():
        ext_a[:, :, 0:PAST_A, :] = ext_a[:, :, tile:tile + PAST_A, :]
        ext_b[:, :, 0:PAST_B, :] = ext_b[:, :, tile:tile + PAST_B, :]
        ext_d[:, :, 0:PAST_D, :] = ext_d[:, :, tile:tile + PAST_D, :]

    lo = lax.broadcasted_iota(jnp.int32, (CHUNK, LANES), 1) < HEAD_DIM
    p = _params(pre_g, w_in, ca_w, ca_b, lna_g, lna_b, cb_w, lnc_g, lnc_b, pl_w, pl_s, w_out, post_g)

    def make_tile(i):
        t_proj, t_vn = proj.at[i], vn_ref.at[i]

        def spatial_prepare():
            for c0 in range(0, tile, CHUNK):
                for h in range(HALVES):
                    vb = t_vn[c0:c0 + CHUNK, _half(h)].astype(jnp.bfloat16)
                    s_even = jnp.dot(sp_w[2 * h], vb, preferred_element_type=jnp.float32)
                    s_odd = jnp.dot(sp_w[2 * h + 1], vb, preferred_element_type=jnp.float32)
                    t_proj[c0:c0 + CHUNK, _slot(SPATIAL_OUT, h)] = (
                        jnp.where(lo, s_even, s_odd) + sp_b[0, :, _half(h)])

        def spatial_fn(r0):
            return [t_proj[r0:r0 + ROWS, _slot(SPATIAL_OUT, h)] for h in range(HALVES)]

        def first_block_cnt():
            return j * tile + lax.broadcasted_iota(jnp.int32, (ROWS, LANES), 0) + 1

        return types.SimpleNamespace(
            x=x_ref.at[i], y=y_ref.at[i], ext_a=ext_a.at[i], ext_b=ext_b.at[i], ext_d=ext_d.at[i],
            proj=t_proj, vn=t_vn, mix=mix.at[i], n_rows=tile, stride=1,
            spatial_prepare=spatial_prepare, spatial_fn=spatial_fn, first_block_cnt=first_block_cnt)

    _run_tiles([make_tile(i) for i in range(n_tiles)], p)

    @pl.when(j == pl.num_programs(1) - 1)
    def _():
        end_a, end_b, end_d = PAST_A + tile, PAST_B + tile, PAST_D + tile
        for i in range(n_tiles):
            for h in range(HALVES):
                ca_out[i, :, _half(h)] = ext_a[i, h, end_a - (CONV_A_WIDTH - 1):end_a, :]
                cb_out[i, :, _half(h)] = ext_b[i, h, end_b - (CONV_B_WIDTH - 1):end_b, :]
                pd_out[i, :, _half(h)] = ext_d[i, h, end_d - POOL_BUF:end_d, :]


def _sample_kernel(x_ref, sa_ref, sb_ref, sd_ref, pre_g, w_in_blk, ca_w, ca_b, lna_g, lna_b, cb_w, lnc_g,
                   lnc_b, sp_w, sp_b, pl_w, pl_s, w_out_blk, post_g,
                   y_ref, ca_out, cb_out, pd_out, cv_out,
                   w_in, w_out, ext_a, ext_b, ext_d, proj, vn_ref, mix, xs_ref, ys_ref, *, n_steps):
    sb = SAMPLE_BLOCK
    n_rows = n_steps * sb
    n_tiles = xs_ref.shape[0]

    @pl.when(pl.program_id(0) == 0)
    def _():
        _copy_rows(w_in, w_in_blk.at[0], D_MODEL, BF16_ROWS)
        _copy_rows(w_out, w_out_blk.at[0], D_MODEL, BF16_ROWS)

    for i in range(n_tiles):
        seqs = slice(i * sb, (i + 1) * sb)
        for s in range(n_steps):
            xs_ref[i, s * sb:(s + 1) * sb, :] = x_ref[s, seqs, :]
        for h in range(HALVES):
            for s in range(CONV_A_WIDTH - 1):
                r = (PAST_A - (CONV_A_WIDTH - 1) + s) * sb
                ext_a[i, h, r:r + sb, :] = sa_ref[0, s, seqs, _half(h)]
            for s in range(CONV_B_WIDTH - 1):
                r = (PAST_B - (CONV_B_WIDTH - 1) + s) * sb
                ext_b[i, h, r:r + sb, :] = sb_ref[0, s, seqs, _half(h)]
            for s in range(POOL_BUF):
                r = (PAST_D - POOL_BUF + s) * sb
                ext_d[i, h, r:r + sb, :] = sd_ref[0, s, seqs, _half(h)]

    p = _params(pre_g, w_in, ca_w, ca_b, lna_g, lna_b, cb_w, lnc_g, lnc_b, pl_w, pl_s, w_out, post_g)

    def make_tile(i):
        t_vn = vn_ref.at[i]

        def spatial_fn(r0):
            s = r0 // sb
            out = []
            for h in range(HALVES):
                acc = jnp.broadcast_to(sp_b[0, s:s + 1, _half(h)], (sb, LANES))
                for s2 in range(s + 1):
                    w_row = s * n_steps + s2
                    acc = acc + sp_w[0, w_row:w_row + 1, _half(h)] * t_vn[s2 * sb:(s2 + 1) * sb, _half(h)]
                out.append(acc)
            return out

        return types.SimpleNamespace(
            x=xs_ref.at[i], y=ys_ref.at[i], ext_a=ext_a.at[i], ext_b=ext_b.at[i], ext_d=ext_d.at[i],
            proj=proj.at[i], vn=t_vn, mix=mix.at[i], n_rows=n_rows, stride=sb,
            spatial_prepare=lambda: None, spatial_fn=spatial_fn, first_block_cnt=lambda: None)

    _run_tiles([make_tile(i) for i in range(n_tiles)], p)

    for i in range(n_tiles):
        seqs = slice(i * sb, (i + 1) * sb)
        for s in range(n_steps):
            y_ref[s, seqs, :] = ys_ref[i, s * sb:(s + 1) * sb, :]
            cv_out[s, seqs, :] = vn_ref[i, s * sb:(s + 1) * sb, :]
        for h in range(HALVES):
            for s in range(CONV_A_WIDTH - 1):
                r = (PAST_A + n_steps - (CONV_A_WIDTH - 1) + s) * sb
                ca_out[s, seqs, _half(h)] = ext_a[i, h, r:r + sb, :]
            for s in range(CONV_B_WIDTH - 1):
                r = (PAST_B + n_steps - (CONV_B_WIDTH - 1) + s) * sb
                cb_out[s, seqs, _half(h)] = ext_b[i, h, r:r + sb, :]
            for s in range(POOL_BUF):
                r = (PAST_D + n_steps - POOL_BUF + s) * sb
                pd_out[s, seqs, _half(h)] = ext_d[i, h, r:r + sb, :]


def _layer_spec(arr, l, single_buffer=False):
    shape = (1,) + arr.shape[1:]
    index_map = lambda *_: (l,) + (0,) * (len(shape) - 1)
    if single_buffer:
        return pl.BlockSpec(shape, index_map, pipeline_mode=pl.Buffered(1))
    return pl.BlockSpec(shape, index_map)


def _scratch(n_tiles, n_rows, stride):
    return [
        pltpu.VMEM((D_MODEL, N_IN), jnp.bfloat16),
        pltpu.VMEM((D_MODEL, D_MODEL), jnp.bfloat16),
    ], [
        pltpu.VMEM((n_tiles, HALVES, PAST_A * stride + n_rows, LANES), jnp.float32),
        pltpu.VMEM((n_tiles, HALVES, PAST_B * stride + n_rows, LANES), jnp.float32),
        pltpu.VMEM((n_tiles, HALVES, PAST_D * stride + n_rows, LANES), jnp.float32),
        pltpu.VMEM((n_tiles, n_rows, N_IN), jnp.float32),
        pltpu.VMEM((n_tiles, n_rows, GROUP), jnp.float32),
        pltpu.VMEM((n_tiles, n_rows, D_MODEL), jnp.bfloat16),
    ]


def _weight_specs(w, l, names):
    return [_layer_spec(w[n], l, single_buffer=n in ("w_in", "w_out")) for n in names]


_PROMPT_WEIGHTS = ("pre_g", "w_in", "ca_w", "ca_b", "lna_g", "lna_b", "cb_w", "lnc_g", "lnc_b", "sp_w",
                   "sp_b_prompt", "pl_w", "pl_s", "w_out", "post_g")
_SAMPLE_WEIGHTS = ("pre_g", "w_in", "ca_w", "ca_b", "lna_g", "lna_b", "cb_w", "lnc_g", "lnc_b", "sp_w_sample",
                   "sp_b_sample", "pl_w", "pl_s", "w_out", "post_g")


def _prompt_layer(x, w, l):
    batch, seq, _ = x.shape
    tile = PROMPT_TILE
    n_tiles = PROMPT_TILES
    assert seq % tile == 0 and tile % CHUNK == 0 and tile >= PAST_A and batch % n_tiles == 0
    tok = pl.BlockSpec((n_tiles, tile, D_MODEL), lambda b, j: (b, j, 0))

    def state(n):
        return pl.BlockSpec((n_tiles, n, GROUP), lambda b, j: (b, 0, 0))

    weight_scratch, tile_scratch = _scratch(n_tiles, tile, 1)
    return pl.pallas_call(
        _prompt_kernel,
        grid=(batch // n_tiles, seq // tile),
        in_specs=[tok] + _weight_specs(w, l, _PROMPT_WEIGHTS),
        out_specs=[tok, state(CONV_A_WIDTH - 1), state(CONV_B_WIDTH - 1), state(POOL_BUF)],
        out_shape=[
            jax.ShapeDtypeStruct(x.shape, x.dtype),
            jax.ShapeDtypeStruct((batch, CONV_A_WIDTH - 1, GROUP), x.dtype),
            jax.ShapeDtypeStruct((batch, CONV_B_WIDTH - 1, GROUP), x.dtype),
            jax.ShapeDtypeStruct((batch, POOL_BUF, GROUP), x.dtype),
        ],
        scratch_shapes=weight_scratch + [pltpu.VMEM((N_HEADS, CHUNK, CHUNK), jnp.bfloat16)] + tile_scratch,
        compiler_params=pltpu.CompilerParams(
            dimension_semantics=("arbitrary", "arbitrary"), vmem_limit_bytes=VMEM_LIMIT),
        name="prompt_layer",
    )(x, *[w[n] for n in _PROMPT_WEIGHTS])


def _sample_layer(x_t, sa_t, sb_t, sd_t, w, l):
    n_steps, batch, _ = x_t.shape
    sb = SAMPLE_BLOCK
    n_tiles = SAMPLE_TILES
    blk = n_tiles * sb
    assert batch % blk == 0 and sb == ROWS

    def tm(n, width):
        return pl.BlockSpec((n, blk, width), lambda i: (0, i, 0))

    def tm_l(n, width):
        return pl.BlockSpec((1, n, blk, width), lambda i: (l, 0, i, 0))

    n_rows = n_steps * sb
    weight_scratch, tile_scratch = _scratch(n_tiles, n_rows, sb)
    return pl.pallas_call(
        functools.partial(_sample_kernel, n_steps=n_steps),
        grid=(batch // blk,),
        in_specs=[tm(n_steps, D_MODEL), tm_l(CONV_A_WIDTH - 1, GROUP), tm_l(CONV_B_WIDTH - 1, GROUP),
                  tm_l(POOL_BUF, GROUP)] + _weight_specs(w, l, _SAMPLE_WEIGHTS),
        out_specs=[tm(n_steps, D_MODEL), tm(CONV_A_WIDTH - 1, GROUP), tm(CONV_B_WIDTH - 1, GROUP),
                   tm(POOL_BUF, GROUP), tm(n_steps, GROUP)],
        out_shape=[
            jax.ShapeDtypeStruct(x_t.shape, x_t.dtype),
            jax.ShapeDtypeStruct(sa_t.shape[1:], x_t.dtype),
            jax.ShapeDtypeStruct(sb_t.shape[1:], x_t.dtype),
            jax.ShapeDtypeStruct(sd_t.shape[1:], x_t.dtype),
            jax.ShapeDtypeStruct((n_steps, batch, GROUP), x_t.dtype),
        ],
        scratch_shapes=weight_scratch + tile_scratch + [
            pltpu.VMEM((n_tiles, n_rows, D_MODEL), jnp.float32),
            pltpu.VMEM((n_tiles, n_rows, D_MODEL), jnp.float32),
        ],
        compiler_params=pltpu.CompilerParams(
            dimension_semantics=("arbitrary",), vmem_limit_bytes=VMEM_LIMIT),
        name="sample_layer",
    )(x_t, sa_t, sb_t, sd_t, *[w[n] for n in _SAMPLE_WEIGHTS])


def _prepare_weights(n_steps, pre_norm_g, w_in, conv_a_w, conv_a_b, ln_a_g, ln_a_b, conv_b_w, ln_c_g, ln_c_b,
                     spatial_w, spatial_b, pool_w, pool_scale, w_out, post_norm_g):
    row = lambda v: v[:, None, :]
    depth = w_in.shape[0]
    eye = jnp.eye(N_HEADS, dtype=pool_w.dtype)
    pool_bd = jnp.einsum("lgcd,gh->lgchd", pool_w, eye).reshape(depth, GROUP, GROUP)
    sw = spatial_w[:, :, :n_steps, :n_steps] * jnp.tril(jnp.ones((n_steps, n_steps), spatial_w.dtype))
    sw = sw.transpose(0, 2, 3, 1).reshape(depth, n_steps * n_steps, N_HEADS)
    return {
        "pre_g": row(pre_norm_g), "w_in": w_in.astype(jnp.bfloat16),
        "ca_w": conv_a_w, "ca_b": row(conv_a_b), "lna_g": row(ln_a_g), "lna_b": row(ln_a_b),
        "cb_w": conv_b_w, "lnc_g": row(ln_c_g), "lnc_b": row(ln_c_b),
        "sp_w": spatial_w,
        "sp_b_prompt": jnp.repeat(spatial_b.transpose(0, 2, 1), HEAD_DIM, axis=2),
        "sp_w_sample": jnp.repeat(sw, HEAD_DIM, axis=2),
        "sp_b_sample": jnp.repeat(spatial_b[:, :, :n_steps].transpose(0, 2, 1), HEAD_DIM, axis=2),
        "pl_w": pool_bd.astype(jnp.bfloat16), "pl_s": row(pool_scale),
        "w_out": w_out.astype(jnp.bfloat16), "post_g": row(post_norm_g),
    }


def kernel(x_prompt, x_sample, state_conv_a, state_conv_b, state_pool, pre_norm_g, w_in, conv_a_w, conv_a_b, ln_a_g, ln_a_b, conv_b_w, ln_c_g, ln_c_b, spatial_w, spatial_b, pool_w, pool_scale, w_out, post_norm_g):
    depth = w_in.shape[0]
    n_steps = x_sample.shape[1]
    to_tm = lambda v: jnp.swapaxes(v, -3, -2)
    w = _prepare_weights(n_steps, pre_norm_g, w_in, conv_a_w, conv_a_b, ln_a_g, ln_a_b, conv_b_w, ln_c_g,
                         ln_c_b, spatial_w, spatial_b, pool_w, pool_scale, w_out, post_norm_g)
    yp, ys = x_prompt, to_tm(x_sample)
    sa, sb, sd = to_tm(state_conv_a), to_tm(state_conv_b), to_tm(state_pool)
    outs = [[] for _ in range(7)]
    for l in range(depth):
        yp, a_p, b_p, d_p = _prompt_layer(yp, w, l)
        ys, a_s, b_s, d_s, v_s = _sample_layer(ys, sa, sb, sd, w, l)
        for o, v in zip(outs, (a_p, a_s, b_p, b_s, d_p, d_s, v_s)):
            o.append(v)
    outs = [jnp.stack(o) for o in outs]
    for k in (1, 3, 5, 6):
        outs[k] = to_tm(outs[k])
    return (yp, to_tm(ys)) + tuple(outs)
```

```python
import functools
import types

import jax
import jax.numpy as jnp
from jax import lax
from jax.experimental import pallas as pl
from jax.experimental.pallas import tpu as pltpu

D_MODEL = 1024
GROUP = 256
HEAD_DIM = 64
N_HEADS = 4
CONV_A_WIDTH = 31
CONV_B_WIDTH = 3
CHUNK = 128
POOL_BUF = 15
EPS = 1e-6
LANES = 128
BF16_ROWS = 16
HALVES = GROUP // LANES

(A_VAL, A_GATE, Z_A, B_B, B_C, B_X, Z_B, C_U, C_V, Z_C, D_X, Z_D) = range(12)
N_IN = 12 * GROUP
SPATIAL_OUT = C_V
POOL_OUT = D_X

PAST_A = 32
PAST_B = 8
PAST_D = 16

PROMPT_TILE = 1024
SAMPLE_SEQS = 64
ROWS = 32

VMEM_LIMIT = 58 * 1024 * 1024


def _sigmoid(v):
    return 1.0 / (1.0 + jnp.exp(-v))


def _silu(v):
    return v * _sigmoid(v)


def _rowsum(vs):
    tot = vs[0]
    for v in vs[1:]:
        tot = tot + v
    return jnp.sum(tot, axis=-1, keepdims=True)


def _half(h):
    return slice(h * LANES, (h + 1) * LANES)


def _slot(slot, h):
    return slice(slot * GROUP + h * LANES, slot * GROUP + (h + 1) * LANES)


def _layernorm2(v0, v1, g_ref, b_ref):
    mu = _rowsum([v0, v1]) * (1.0 / GROUP)
    d0, d1 = v0 - mu, v1 - mu
    var = _rowsum([d0 * d0, d1 * d1]) * (1.0 / GROUP)
    inv = lax.rsqrt(var + EPS)
    return (d0 * inv * g_ref[:, _half(0)] + b_ref[:, _half(0)],
            d1 * inv * g_ref[:, _half(1)] + b_ref[:, _half(1)])


def _copy_rows(dst, src, n_rows, chunk):
    def body(i, carry):
        r = pl.multiple_of(i * chunk, chunk)
        dst[pl.ds(r, chunk), :] = src[pl.ds(r, chunk), :].astype(dst.dtype)
        return carry
    lax.fori_loop(0, n_rows // chunk, body, 0)


class _History:
    def __init__(self, ref, past, seq_len):
        self.ref, self.past, self.seq_len = ref, past, seq_len
        self.piece = min(seq_len, ROWS)

    def _offsets(self, r0):
        for i in range(ROWS // self.piece):
            run, step = divmod(r0 + i * self.piece, self.seq_len)
            yield i, run * (self.past + self.seq_len) + self.past + step

    def window(self, h, r0, back, extra=0):
        assert extra == 0 or self.piece == ROWS
        parts = [self.ref[h, off - back - extra:off - back + self.piece, :] for _, off in self._offsets(r0)]
        return parts[0] if len(parts) == 1 else jnp.concatenate(parts, axis=0)

    def store(self, h, r0, value):
        for i, off in self._offsets(r0):
            self.ref[h, off:off + self.piece, :] = value[i * self.piece:(i + 1) * self.piece]

    def tail(self, h, run, n):
        end = (run + 1) * (self.past + self.seq_len)
        return self.ref[h, end - n:end, :]

    def set_history(self, h, run, value):
        start = run * (self.past + self.seq_len) + self.past
        self.ref[h, start - value.shape[0]:start, :] = value


def _phase_in(t, p):
    for r0 in range(0, t.n_rows, ROWS):
        xb = t.x[r0:r0 + ROWS, :]
        ms = jnp.sum(xb * xb, axis=-1, keepdims=True) * (1.0 / D_MODEL)
        hb = xb * lax.rsqrt(ms + EPS) * p.pre_g[...]
        t.mix[r0:r0 + ROWS, :] = hb.astype(jnp.bfloat16)
    t.proj[...] = jnp.dot(t.mix[...], p.w_in[...], preferred_element_type=jnp.float32)


def _phase_mix(t, p):
    lo = lax.broadcasted_iota(jnp.int32, (ROWS, LANES), 1) < HEAD_DIM
    inv_w = [jnp.where(lo, 1.0 / 2, 1.0 / 4), jnp.where(lo, 1.0 / 8, 1.0 / 16)]
    win = [jnp.where(lo, 2, 4), jnp.where(lo, 8, 16)]
    hist_a, hist_b, hist_d = t.hist_a, t.hist_b, t.hist_d

    def pj(r0, slot, h):
        return t.proj[r0:r0 + ROWS, _slot(slot, h)]

    for r0 in range(0, t.n_rows, ROWS):
        for h in range(HALVES):
            hist_a.store(h, r0, pj(r0, A_VAL, h) * _sigmoid(pj(r0, A_GATE, h)))
            hist_b.store(h, r0, pj(r0, B_C, h) * pj(r0, B_X, h))
            hist_d.store(h, r0, pj(r0, D_X, h))
        v0, v1 = _layernorm2(pj(r0, C_V, 0), pj(r0, C_V, 1), p.lnc_g, p.lnc_b)
        t.vn[r0:r0 + ROWS, _half(0)] = v0
        t.vn[r0:r0 + ROWS, _half(1)] = v1

        a2 = hist_d.window(0, r0, 0) + hist_d.window(0, r0, 1)
        a4 = a2 + (hist_d.window(0, r0, 2) + hist_d.window(0, r0, 3))
        sums = [jnp.where(lo, a2, a4)]
        if hist_d.piece == ROWS:
            s8 = hist_d.window(1, r0, 0, extra=8)
            for j in range(1, 8):
                s8 = s8 + hist_d.window(1, r0, j, extra=8)
            s8_cur, s8_prev = s8[8:], s8[:ROWS]
        else:
            s8_cur, s8_prev = hist_d.window(1, r0, 0), hist_d.window(1, r0, 8)
            for j in range(1, 8):
                s8_cur = s8_cur + hist_d.window(1, r0, j)
                s8_prev = s8_prev + hist_d.window(1, r0, 8 + j)
        sums.append(jnp.where(lo, s8_cur, s8_cur + s8_prev))
        cnt = t.first_block_cnt() if r0 == 0 else None
        for h in range(HALVES):
            if cnt is None:
                mean = sums[h] * inv_w[h]
            else:
                mean = sums[h] / jnp.minimum(cnt, win[h]).astype(jnp.float32)
            t.mix[r0:r0 + ROWS, _slot(3, h)] = (mean - hist_d.window(h, r0, 0)).astype(jnp.bfloat16)

    lo_chunk = lax.broadcasted_iota(jnp.int32, (CHUNK, LANES), 1) < HEAD_DIM
    for c0 in range(0, t.n_rows, CHUNK):
        for h in range(HALVES):
            vb = t.vn[c0:c0 + CHUNK, _half(h)].astype(jnp.bfloat16)
            s_even = jnp.dot(p.sp_w[2 * h], vb, preferred_element_type=jnp.float32)
            s_odd = jnp.dot(p.sp_w[2 * h + 1], vb, preferred_element_type=jnp.float32)
            t.proj[c0:c0 + CHUNK, _slot(SPATIAL_OUT, h)] = (
                jnp.where(lo_chunk, s_even, s_odd) + p.sp_b[:, _half(h)])
    t.proj[:, POOL_OUT * GROUP:(POOL_OUT + 1) * GROUP] = jnp.dot(
        t.mix[:, 3 * GROUP:4 * GROUP], p.pl_w[...], preferred_element_type=jnp.float32)

    for r0 in range(0, t.n_rows, ROWS):
        ya = []
        for h in range(HALVES):
            acc = jnp.broadcast_to(p.ca_b[:, _half(h)], (ROWS, LANES))
            for k in range(CONV_A_WIDTH):
                acc = acc + p.ca_w[k:k + 1, _half(h)] * hist_a.window(h, r0, CONV_A_WIDTH - 1 - k)
            ya.append(acc)
        na = _layernorm2(ya[0], ya[1], p.lna_g, p.lna_b)
        for h in range(HALVES):
            t.mix[r0:r0 + ROWS, _slot(0, h)] = (_silu(na[h]) * _silu(pj(r0, Z_A, h))).astype(jnp.bfloat16)

        for h in range(HALVES):
            acc = None
            for k in range(CONV_B_WIDTH):
                term = p.cb_w[k:k + 1, _half(h)] * hist_b.window(h, r0, CONV_B_WIDTH - 1 - k)
                acc = term if acc is None else acc + term
            yb = pj(r0, B_B, h) * acc * _silu(pj(r0, Z_B, h))
            t.mix[r0:r0 + ROWS, _slot(1, h)] = yb.astype(jnp.bfloat16)

        for h in range(HALVES):
            yc = pj(r0, C_U, h) * pj(r0, SPATIAL_OUT, h) * _silu(pj(r0, Z_C, h))
            t.mix[r0:r0 + ROWS, _slot(2, h)] = yc.astype(jnp.bfloat16)
            yd = pj(r0, POOL_OUT, h) * p.pl_s[:, _half(h)] * _silu(pj(r0, Z_D, h))
            t.mix[r0:r0 + ROWS, _slot(3, h)] = yd.astype(jnp.bfloat16)

    t.proj[:, 0:D_MODEL] = jnp.dot(t.mix[...], p.w_out[...], preferred_element_type=jnp.float32)


def _phase_out(t, p):
    for r0 in range(0, t.n_rows, ROWS):
        ob = t.proj[r0:r0 + ROWS, 0:D_MODEL]
        ms = jnp.sum(ob * ob, axis=-1, keepdims=True) * (1.0 / D_MODEL)
        t.y[r0:r0 + ROWS, :] = t.x[r0:r0 + ROWS, :] + ob * lax.rsqrt(ms + EPS) * p.post_g[...]


_WEIGHTS = ("pre_g", "w_in", "ca_w", "ca_b", "lna_g", "lna_b", "cb_w", "lnc_g", "lnc_b", "sp_w", "sp_b",
            "pl_w", "pl_s", "w_out", "post_g")
_STATES = ((CONV_A_WIDTH - 1, PAST_A), (CONV_B_WIDTH - 1, PAST_B), (POOL_BUF, PAST_D))


def _layer_kernel(*refs, sample, n_runs, seq_len):
    n_in = 1 + (len(_STATES) if sample else 0) + len(_WEIGHTS)
    n_out = 1 + len(_STATES) + (1 if sample else 0)
    x_ref, state_in, w = refs[0], refs[1:n_in - len(_WEIGHTS)], dict(zip(_WEIGHTS, refs[n_in - len(_WEIGHTS):n_in]))
    y_ref, state_out = refs[n_in], refs[n_in + 1:n_in + 1 + len(_STATES)]
    w_in, w_out, sp_w, ext_a, ext_b, ext_d, proj, vn_ref, mix = refs[n_in + n_out:]
    n_rows = n_runs * seq_len
    hists = [_History(ext, past, seq_len) for ext, (_, past) in zip((ext_a, ext_b, ext_d), _STATES)]

    first = pl.program_id(0) == 0
    if not sample:
        j = pl.program_id(1)
        first = first & (j == 0)

    @pl.when(first)
    def _():
        _copy_rows(w_in, w["w_in"].at[0], D_MODEL, BF16_ROWS)
        _copy_rows(w_out, w["w_out"].at[0], D_MODEL, BF16_ROWS)
        keep = (lax.broadcasted_iota(jnp.int32, (CHUNK, CHUNK), 1)
                <= lax.broadcasted_iota(jnp.int32, (CHUNK, CHUNK), 0))
        for hd in range(N_HEADS):
            sp_w[hd] = jnp.where(keep, w["sp_w"][0, hd], 0.0).astype(jnp.bfloat16)

    if sample:
        for hist, ref, (n, _) in zip(hists, state_in, _STATES):
            for run in range(n_runs):
                for h in range(HALVES):
                    hist.set_history(h, run, ref[0, run, :, _half(h)])
        first_block_cnt = lambda: None
    else:
        @pl.when(j == 0)
        def _():
            for hist in hists:
                hist.ref[:, 0:hist.past, :] = jnp.zeros((HALVES, hist.past, LANES), jnp.float32)

        @pl.when(j > 0)
        def _():
            for hist in hists:
                hist.ref[:, 0:hist.past, :] = hist.ref[:, seq_len:seq_len + hist.past, :]

        def first_block_cnt():
            return j * seq_len + lax.broadcasted_iota(jnp.int32, (ROWS, LANES), 0) + 1

    row = lambda name: w[name].at[0]
    p = types.SimpleNamespace(
        pre_g=row("pre_g"), w_in=w_in, ca_w=row("ca_w"), ca_b=row("ca_b"), lna_g=row("lna_g"),
        lna_b=row("lna_b"), cb_w=row("cb_w"), lnc_g=row("lnc_g"), lnc_b=row("lnc_b"), sp_w=sp_w,
        sp_b=row("sp_b"), pl_w=row("pl_w"), pl_s=row("pl_s"), w_out=w_out, post_g=row("post_g"))
    t = types.SimpleNamespace(
        x=x_ref if sample else x_ref.at[0], y=y_ref if sample else y_ref.at[0],
        hist_a=hists[0], hist_b=hists[1], hist_d=hists[2], proj=proj, vn=vn_ref, mix=mix,
        n_rows=n_rows, first_block_cnt=first_block_cnt)
    _phase_in(t, p)
    _phase_mix(t, p)
    _phase_out(t, p)

    def write_states():
        for hist, ref, (n, _) in zip(hists, state_out, _STATES):
            for run in range(n_runs):
                for h in range(HALVES):
                    ref[run, :, _half(h)] = hist.tail(h, run, n)

    if sample:
        write_states()
        refs[n_in + n_out - 1][...] = vn_ref[...]
    else:
        pl.when(j == pl.num_programs(1) - 1)(write_states)


def _layer_spec(arr, l, single_buffer=False):
    shape = (1,) + arr.shape[1:]
    index_map = lambda *_: (l,) + (0,) * (len(shape) - 1)
    if single_buffer:
        return pl.BlockSpec(shape, index_map, pipeline_mode=pl.Buffered(1))
    return pl.BlockSpec(shape, index_map)


def _scratch(n_runs, seq_len):
    n_rows = n_runs * seq_len
    return [
        pltpu.VMEM((D_MODEL, N_IN), jnp.bfloat16),
        pltpu.VMEM((D_MODEL, D_MODEL), jnp.bfloat16),
        pltpu.VMEM((N_HEADS, CHUNK, CHUNK), jnp.bfloat16),
    ] + [
        pltpu.VMEM((HALVES, n_runs * (past + seq_len), LANES), jnp.float32) for _, past in _STATES
    ] + [
        pltpu.VMEM((n_rows, N_IN), jnp.float32),
        pltpu.VMEM((n_rows, GROUP), jnp.float32),
        pltpu.VMEM((n_rows, D_MODEL), jnp.bfloat16),
    ]


def _weight_args(w, l, sp_names):
    names = [sp_names.get(n, n) for n in _WEIGHTS]
    specs = [_layer_spec(w[n], l, single_buffer=n in ("w_in", "w_out")) for n in names]
    return specs, [w[n] for n in names]


def _prompt_layer(x, w, l):
    batch, seq, _ = x.shape
    tile = PROMPT_TILE
    assert seq % tile == 0 and tile % CHUNK == 0 and tile >= PAST_A
    tok = pl.BlockSpec((1, tile, D_MODEL), lambda b, j: (b, j, 0))
    w_specs, w_args = _weight_args(w, l, {"sp_w": "sp_w_prompt", "sp_b": "sp_b_prompt"})
    return pl.pallas_call(
        functools.partial(_layer_kernel, sample=False, n_runs=1, seq_len=tile),
        grid=(batch, seq // tile),
        in_specs=[tok] + w_specs,
        out_specs=[tok] + [pl.BlockSpec((1, n, GROUP), lambda b, j: (b, 0, 0)) for n, _ in _STATES],
        out_shape=[jax.ShapeDtypeStruct(x.shape, x.dtype)]
        + [jax.ShapeDtypeStruct((batch, n, GROUP), x.dtype) for n, _ in _STATES],
        scratch_shapes=_scratch(1, tile),
        compiler_params=pltpu.CompilerParams(
            dimension_semantics=("arbitrary", "arbitrary"), vmem_limit_bytes=VMEM_LIMIT),
        name="prompt_layer",
    )(x, *w_args)


def _sample_layer(x2d, states, w, l, n_steps):
    batch = x2d.shape[0] // n_steps
    seqs = SAMPLE_SEQS
    assert batch % seqs == 0 and ROWS % n_steps == 0 and (seqs * n_steps) % CHUNK == 0 and CHUNK % n_steps == 0
    rows = seqs * n_steps
    w_specs, w_args = _weight_args(w, l, {"sp_w": "sp_w_sample", "sp_b": "sp_b_sample"})
    return pl.pallas_call(
        functools.partial(_layer_kernel, sample=True, n_runs=seqs, seq_len=n_steps),
        grid=(batch // seqs,),
        in_specs=[pl.BlockSpec((rows, D_MODEL), lambda i: (i, 0))]
        + [pl.BlockSpec((1, seqs, n, GROUP), lambda i: (l, i, 0, 0)) for n, _ in _STATES] + w_specs,
        out_specs=[pl.BlockSpec((rows, D_MODEL), lambda i: (i, 0))]
        + [pl.BlockSpec((seqs, n, GROUP), lambda i: (i, 0, 0)) for n, _ in _STATES]
        + [pl.BlockSpec((rows, GROUP), lambda i: (i, 0))],
        out_shape=[jax.ShapeDtypeStruct(x2d.shape, x2d.dtype)]
        + [jax.ShapeDtypeStruct((batch, n, GROUP), x2d.dtype) for n, _ in _STATES]
        + [jax.ShapeDtypeStruct((batch * n_steps, GROUP), x2d.dtype)],
        scratch_shapes=_scratch(seqs, n_steps),
        compiler_params=pltpu.CompilerParams(
            dimension_semantics=("arbitrary",), vmem_limit_bytes=VMEM_LIMIT),
        name="sample_layer",
    )(x2d, *states, *w_args)


def _prepare_weights(n_steps, pre_norm_g, w_in, conv_a_w, conv_a_b, ln_a_g, ln_a_b, conv_b_w, ln_c_g, ln_c_b,
                     spatial_w, spatial_b, pool_w, pool_scale, w_out, post_norm_g):
    row = lambda v: v[:, None, :]
    depth = w_in.shape[0]
    eye = jnp.eye(N_HEADS, dtype=pool_w.dtype)
    pool_bd = jnp.einsum("lgcd,gh->lgchd", pool_w, eye).reshape(depth, GROUP, GROUP)
    reps = CHUNK // n_steps
    corner = spatial_w[:, :, :n_steps, :n_steps]
    sw_sample = jnp.einsum("lhts,qr->lhqtrs", corner, jnp.eye(reps, dtype=corner.dtype)).reshape(
        depth, N_HEADS, CHUNK, CHUNK)
    sb_prompt = jnp.repeat(spatial_b.transpose(0, 2, 1), HEAD_DIM, axis=2)
    return {
        "pre_g": row(pre_norm_g), "w_in": w_in.astype(jnp.bfloat16),
        "ca_w": conv_a_w, "ca_b": row(conv_a_b), "lna_g": row(ln_a_g), "lna_b": row(ln_a_b),
        "cb_w": conv_b_w, "lnc_g": row(ln_c_g), "lnc_b": row(ln_c_b),
        "sp_w_prompt": spatial_w, "sp_b_prompt": sb_prompt,
        "sp_w_sample": sw_sample, "sp_b_sample": jnp.tile(sb_prompt[:, :n_steps], (1, reps, 1)),
        "pl_w": pool_bd.astype(jnp.bfloat16), "pl_s": row(pool_scale),
        "w_out": w_out.astype(jnp.bfloat16), "post_g": row(post_norm_g),
    }


def kernel(x_prompt, x_sample, state_conv_a, state_conv_b, state_pool, pre_norm_g, w_in, conv_a_w, conv_a_b, ln_a_g, ln_a_b, conv_b_w, ln_c_g, ln_c_b, spatial_w, spatial_b, pool_w, pool_scale, w_out, post_norm_g):
    depth = w_in.shape[0]
    dec_batch, n_steps, _ = x_sample.shape
    w = _prepare_weights(n_steps, pre_norm_g, w_in, conv_a_w, conv_a_b, ln_a_g, ln_a_b, conv_b_w, ln_c_g,
                         ln_c_b, spatial_w, spatial_b, pool_w, pool_scale, w_out, post_norm_g)
    yp, ys = x_prompt, x_sample.reshape(dec_batch * n_steps, D_MODEL)
    states = (state_conv_a, state_conv_b, state_pool)
    outs = [[] for _ in range(7)]
    for l in range(depth):
        yp, a_p, b_p, d_p = _prompt_layer(yp, w, l)
        ys, a_s, b_s, d_s, v_s = _sample_layer(ys, states, w, l, n_steps)
        for o, v in zip(outs, (a_p, a_s, b_p, b_s, d_p, d_s, v_s.reshape(dec_batch, n_steps, GROUP))):
            o.append(v)
    return (yp, ys.reshape(x_sample.shape)) + tuple(jnp.stack(o) for o in outs)
```

```python
import functools
import types

import jax
import jax.numpy as jnp
from jax import lax
from jax.experimental import pallas as pl
from jax.experimental.pallas import tpu as pltpu

D_MODEL = 1024
GROUP = 256
HEAD_DIM = 64
N_HEADS = 4
CONV_A_WIDTH = 31
CONV_B_WIDTH = 3
CHUNK = 128
POOL_BUF = 15
EPS = 1e-6
LANES = 128
BF16_ROWS = 16
HALVES = GROUP // LANES

(A_VAL, A_GATE, Z_A, B_B, B_C, B_X, Z_B, C_U, C_V, Z_C, D_X, Z_D) = range(12)
N_IN = 12 * GROUP
SPATIAL_OUT = C_V
POOL_OUT = D_X

PAST_A = 32
PAST_B = 8
PAST_D = 16

PROMPT_TILE = 1024
SAMPLE_SEQS = 64
ROWS = 32

VMEM_LIMIT = 58 * 1024 * 1024


def _sigmoid(v):
    return 1.0 / (1.0 + jnp.exp(-v))


def _silu(v):
    return v * _sigmoid(v)


def _rowsum(vs):
    tot = vs[0]
    for v in vs[1:]:
        tot = tot + v
    return jnp.sum(tot, axis=-1, keepdims=True)


def _half(h):
    return slice(h * LANES, (h + 1) * LANES)


def _slot(slot, h):
    return slice(slot * GROUP + h * LANES, slot * GROUP + (h + 1) * LANES)


def _layernorm2(v0, v1, g_ref, b_ref):
    mu = _rowsum([v0, v1]) * (1.0 / GROUP)
    d0, d1 = v0 - mu, v1 - mu
    var = _rowsum([d0 * d0, d1 * d1]) * (1.0 / GROUP)
    inv = lax.rsqrt(var + EPS)
    return (d0 * inv * g_ref[:, _half(0)] + b_ref[:, _half(0)],
            d1 * inv * g_ref[:, _half(1)] + b_ref[:, _half(1)])


def _copy_rows(dst, src, n_rows, chunk):
    def body(i, carry):
        r = pl.multiple_of(i * chunk, chunk)
        dst[pl.ds(r, chunk), :] = src[pl.ds(r, chunk), :].astype(dst.dtype)
        return carry
    lax.fori_loop(0, n_rows // chunk, body, 0)


class _History:
    def __init__(self, ref, past, seq_len):
        self.ref, self.past, self.seq_len = ref, past, seq_len
        self.piece = min(seq_len, ROWS)

    def _offsets(self, r0):
        for i in range(ROWS // self.piece):
            run, step = divmod(r0 + i * self.piece, self.seq_len)
            yield i, run * (self.past + self.seq_len) + self.past + step

    def window(self, h, r0, back, extra=0):
        assert extra == 0 or self.piece == ROWS
        parts = [self.ref[h, off - back - extra:off - back + self.piece, :] for _, off in self._offsets(r0)]
        return parts[0] if len(parts) == 1 else jnp.concatenate(parts, axis=0)

    def store(self, h, r0, value):
        for i, off in self._offsets(r0):
            self.ref[h, off:off + self.piece, :] = value[i * self.piece:(i + 1) * self.piece]

    def tail(self, h, run, n):
        end = (run + 1) * (self.past + self.seq_len)
        return self.ref[h, end - n:end, :]

    def set_history(self, h, run, value):
        start = run * (self.past + self.seq_len) + self.past
        self.ref[h, start - value.shape[0]:start, :] = value

    def _across_runs(self, row, n_runs):
        return pl.ds(row, n_runs, stride=self.past + self.seq_len)

    def tail_step(self, h, n_runs, j, n):
        return self.ref.at[h][self._across_runs(self.past + self.seq_len - n + j, n_runs), :]

    def set_history_step(self, h, j, n, value):
        self.ref.at[h][self._across_runs(self.past - n + j, value.shape[0]), :] = value


def _phase_in(t, p):
    for r0 in range(0, t.n_rows, ROWS):
        xb = t.x[r0:r0 + ROWS, :]
        ms = jnp.sum(xb * xb, axis=-1, keepdims=True) * (1.0 / D_MODEL)
        hb = xb * lax.rsqrt(ms + EPS) * p.pre_g[...]
        t.mix[r0:r0 + ROWS, :] = hb.astype(jnp.bfloat16)
    t.proj[...] = jnp.dot(t.mix[...], p.w_in[...], preferred_element_type=jnp.float32)


def _phase_mix(t, p):
    lo = lax.broadcasted_iota(jnp.int32, (ROWS, LANES), 1) < HEAD_DIM
    inv_w = [jnp.where(lo, 1.0 / 2, 1.0 / 4), jnp.where(lo, 1.0 / 8, 1.0 / 16)]
    win = [jnp.where(lo, 2, 4), jnp.where(lo, 8, 16)]
    hist_a, hist_b, hist_d = t.hist_a, t.hist_b, t.hist_d

    def pj(r0, slot, h):
        return t.proj[r0:r0 + ROWS, _slot(slot, h)]

    for r0 in range(0, t.n_rows, ROWS):
        for h in range(HALVES):
            hist_a.store(h, r0, pj(r0, A_VAL, h) * _sigmoid(pj(r0, A_GATE, h)))
            hist_b.store(h, r0, pj(r0, B_C, h) * pj(r0, B_X, h))
            hist_d.store(h, r0, pj(r0, D_X, h))
        v0, v1 = _layernorm2(pj(r0, C_V, 0), pj(r0, C_V, 1), p.lnc_g, p.lnc_b)
        t.vn[r0:r0 + ROWS, _half(0)] = v0
        t.vn[r0:r0 + ROWS, _half(1)] = v1

        a2 = hist_d.window(0, r0, 0) + hist_d.window(0, r0, 1)
        a4 = a2 + (hist_d.window(0, r0, 2) + hist_d.window(0, r0, 3))
        sums = [jnp.where(lo, a2, a4)]
        if hist_d.piece == ROWS:
            s8 = hist_d.window(1, r0, 0, extra=8)
            for j in range(1, 8):
                s8 = s8 + hist_d.window(1, r0, j, extra=8)
            s8_cur, s8_prev = s8[8:], s8[:ROWS]
        else:
            s8_cur, s8_prev = hist_d.window(1, r0, 0), hist_d.window(1, r0, 8)
            for j in range(1, 8):
                s8_cur = s8_cur + hist_d.window(1, r0, j)
                s8_prev = s8_prev + hist_d.window(1, r0, 8 + j)
        sums.append(jnp.where(lo, s8_cur, s8_cur + s8_prev))
        cnt = t.first_block_cnt() if r0 == 0 else None
        for h in range(HALVES):
            if cnt is None:
                mean = sums[h] * inv_w[h]
            else:
                mean = sums[h] / jnp.minimum(cnt, win[h]).astype(jnp.float32)
            t.mix[r0:r0 + ROWS, _slot(3, h)] = (mean - hist_d.window(h, r0, 0)).astype(jnp.bfloat16)

    lo_chunk = lax.broadcasted_iota(jnp.int32, (CHUNK, LANES), 1) < HEAD_DIM
    for c0 in range(0, t.n_rows, CHUNK):
        for h in range(HALVES):
            vb = t.vn[c0:c0 + CHUNK, _half(h)].astype(jnp.bfloat16)
            s_even = jnp.dot(p.sp_w[2 * h], vb, preferred_element_type=jnp.float32)
            s_odd = jnp.dot(p.sp_w[2 * h + 1], vb, preferred_element_type=jnp.float32)
            t.proj[c0:c0 + CHUNK, _slot(SPATIAL_OUT, h)] = (
                jnp.where(lo_chunk, s_even, s_odd) + p.sp_b[:, _half(h)])
    t.proj[:, POOL_OUT * GROUP:(POOL_OUT + 1) * GROUP] = jnp.dot(
        t.mix[:, 3 * GROUP:4 * GROUP], p.pl_w[...], preferred_element_type=jnp.float32)

    for r0 in range(0, t.n_rows, ROWS):
        ya = []
        for h in range(HALVES):
            acc = jnp.broadcast_to(p.ca_b[:, _half(h)], (ROWS, LANES))
            for k in range(CONV_A_WIDTH):
                acc = acc + p.ca_w[k:k + 1, _half(h)] * hist_a.window(h, r0, CONV_A_WIDTH - 1 - k)
            ya.append(acc)
        na = _layernorm2(ya[0], ya[1], p.lna_g, p.lna_b)
        for h in range(HALVES):
            t.mix[r0:r0 + ROWS, _slot(0, h)] = (_silu(na[h]) * _silu(pj(r0, Z_A, h))).astype(jnp.bfloat16)

        for h in range(HALVES):
            acc = None
            for k in range(CONV_B_WIDTH):
                term = p.cb_w[k:k + 1, _half(h)] * hist_b.window(h, r0, CONV_B_WIDTH - 1 - k)
                acc = term if acc is None else acc + term
            yb = pj(r0, B_B, h) * acc * _silu(pj(r0, Z_B, h))
            t.mix[r0:r0 + ROWS, _slot(1, h)] = yb.astype(jnp.bfloat16)

        for h in range(HALVES):
            yc = pj(r0, C_U, h) * pj(r0, SPATIAL_OUT, h) * _silu(pj(r0, Z_C, h))
            t.mix[r0:r0 + ROWS, _slot(2, h)] = yc.astype(jnp.bfloat16)
            yd = pj(r0, POOL_OUT, h) * p.pl_s[:, _half(h)] * _silu(pj(r0, Z_D, h))
            t.mix[r0:r0 + ROWS, _slot(3, h)] = yd.astype(jnp.bfloat16)

    t.proj[:, 0:D_MODEL] = jnp.dot(t.mix[...], p.w_out[...], preferred_element_type=jnp.float32)


def _phase_out(t, p):
    for r0 in range(0, t.n_rows, ROWS):
        ob = t.proj[r0:r0 + ROWS, 0:D_MODEL]
        ms = jnp.sum(ob * ob, axis=-1, keepdims=True) * (1.0 / D_MODEL)
        t.y[r0:r0 + ROWS, :] = t.x[r0:r0 + ROWS, :] + ob * lax.rsqrt(ms + EPS) * p.post_g[...]


_WEIGHTS = ("pre_g", "w_in", "ca_w", "ca_b", "lna_g", "lna_b", "cb_w", "lnc_g", "lnc_b", "sp_w", "sp_b",
            "pl_w", "pl_s", "w_out", "post_g")
_STATES = ((CONV_A_WIDTH - 1, PAST_A, True), (CONV_B_WIDTH - 1, PAST_B, False), (POOL_BUF, PAST_D, True))


def _layer_kernel(*refs, sample, n_runs, seq_len, n_carried):
    n_state, n_w = len(_STATES), len(_WEIGHTS)
    n_in = 1 + (n_state if sample else 0) + n_w + n_carried
    n_out = 1 + n_state + (1 if sample else 0)
    x_ref, state_in = refs[0], refs[1:n_in - n_w - n_carried]
    w = dict(zip(_WEIGHTS, refs[n_in - n_w - n_carried:n_in - n_carried]))
    y_ref, state_out = refs[n_in], refs[n_in + 1:n_in + 1 + n_state]
    w_in, w_out, sp_w, ext_a, ext_b, ext_d, proj, vn_ref, mix = refs[n_in + n_out:]
    n_rows = n_runs * seq_len
    hists = [_History(ext, past, seq_len) for ext, (_, past, _) in zip((ext_a, ext_b, ext_d), _STATES)]

    first = pl.program_id(0) == 0
    if not sample:
        j = pl.program_id(1)
        first = first & (j == 0)

    @pl.when(first)
    def _():
        _copy_rows(w_in, w["w_in"].at[0], D_MODEL, BF16_ROWS)
        _copy_rows(w_out, w["w_out"].at[0], D_MODEL, BF16_ROWS)
        keep = (lax.broadcasted_iota(jnp.int32, (CHUNK, CHUNK), 1)
                <= lax.broadcasted_iota(jnp.int32, (CHUNK, CHUNK), 0))
        for hd in range(N_HEADS):
            sp_w[hd] = jnp.where(keep, w["sp_w"][0, hd], 0.0).astype(jnp.bfloat16)

    if sample:
        for hist, ref, (n, _, step_major) in zip(hists, state_in, _STATES):
            for h in range(HALVES):
                if step_major:
                    for s in range(n):
                        hist.set_history_step(h, s, n, ref[0, s, :, _half(h)])
                else:
                    for run in range(n_runs):
                        hist.set_history(h, run, ref[0, run, :, _half(h)])
        first_block_cnt = lambda: None
    else:
        @pl.when(j == 0)
        def _():
            for hist in hists:
                hist.ref[:, 0:hist.past, :] = jnp.zeros((HALVES, hist.past, LANES), jnp.float32)

        @pl.when(j > 0)
        def _():
            for hist in hists:
                hist.ref[:, 0:hist.past, :] = hist.ref[:, seq_len:seq_len + hist.past, :]

        def first_block_cnt():
            return j * seq_len + lax.broadcasted_iota(jnp.int32, (ROWS, LANES), 0) + 1

    row = lambda name: w[name].at[0]
    p = types.SimpleNamespace(
        pre_g=row("pre_g"), w_in=w_in, ca_w=row("ca_w"), ca_b=row("ca_b"), lna_g=row("lna_g"),
        lna_b=row("lna_b"), cb_w=row("cb_w"), lnc_g=row("lnc_g"), lnc_b=row("lnc_b"), sp_w=sp_w,
        sp_b=row("sp_b"), pl_w=row("pl_w"), pl_s=row("pl_s"), w_out=w_out, post_g=row("post_g"))
    t = types.SimpleNamespace(
        x=x_ref if sample else x_ref.at[0], y=y_ref if sample else y_ref.at[0],
        hist_a=hists[0], hist_b=hists[1], hist_d=hists[2], proj=proj, vn=vn_ref, mix=mix,
        n_rows=n_rows, first_block_cnt=first_block_cnt)
    _phase_in(t, p)
    _phase_mix(t, p)
    _phase_out(t, p)

    def write_states():
        for hist, ref, (n, _, step_major) in zip(hists, state_out, _STATES):
            for h in range(HALVES):
                if sample and step_major:
                    for s in range(n):
                        ref[0, s, :, _half(h)] = hist.tail_step(h, n_runs, s, n)
                else:
                    for run in range(n_runs):
                        ref[0, run, :, _half(h)] = hist.tail(h, run, n)

    if sample:
        write_states()
        refs[n_in + n_out - 1][0] = vn_ref[...]
    else:
        pl.when(j == pl.num_programs(1) - 1)(write_states)


def _layer_spec(arr, l, single_buffer=False):
    shape = (1,) + arr.shape[1:]
    index_map = lambda *_: (l,) + (0,) * (len(shape) - 1)
    if single_buffer:
        return pl.BlockSpec(shape, index_map, pipeline_mode=pl.Buffered(1))
    return pl.BlockSpec(shape, index_map)


def _scratch(n_runs, seq_len):
    n_rows = n_runs * seq_len
    return [
        pltpu.VMEM((D_MODEL, N_IN), jnp.bfloat16),
        pltpu.VMEM((D_MODEL, D_MODEL), jnp.bfloat16),
        pltpu.VMEM((N_HEADS, CHUNK, CHUNK), jnp.bfloat16),
    ] + [
        pltpu.VMEM((HALVES, n_runs * (past + seq_len), LANES), jnp.float32) for _, past, _ in _STATES
    ] + [
        pltpu.VMEM((n_rows, N_IN), jnp.float32),
        pltpu.VMEM((n_rows, GROUP), jnp.float32),
        pltpu.VMEM((n_rows, D_MODEL), jnp.bfloat16),
    ]


def _weight_args(w, l, sp_names):
    names = [sp_names.get(n, n) for n in _WEIGHTS]
    specs = [_layer_spec(w[n], l, single_buffer=n in ("w_in", "w_out")) for n in names]
    return specs, [w[n] for n in names]


def _carried(prev, n_inputs):
    specs = [pl.BlockSpec(memory_space=pl.ANY) for _ in prev]
    return specs, list(prev), {n_inputs + k: 1 + k for k in range(len(prev))}


def _prompt_layer(x, w, l, depth, prev):
    batch, seq, _ = x.shape
    tile = PROMPT_TILE
    assert seq % tile == 0 and tile % CHUNK == 0 and tile >= PAST_A
    tok = pl.BlockSpec((1, tile, D_MODEL), lambda b, j: (b, j, 0))
    w_specs, w_args = _weight_args(w, l, {"sp_w": "sp_w_prompt", "sp_b": "sp_b_prompt"})
    c_specs, c_args, aliases = _carried(prev, 1 + len(w_args))
    return pl.pallas_call(
        functools.partial(_layer_kernel, sample=False, n_runs=1, seq_len=tile, n_carried=len(prev)),
        grid=(batch, seq // tile),
        in_specs=[tok] + w_specs + c_specs,
        out_specs=[tok] + [pl.BlockSpec((1, 1, n, GROUP), lambda b, j: (l, b, 0, 0)) for n, _, _ in _STATES],
        out_shape=[jax.ShapeDtypeStruct(x.shape, x.dtype)]
        + [jax.ShapeDtypeStruct((depth, batch, n, GROUP), x.dtype) for n, _, _ in _STATES],
        input_output_aliases=aliases,
        scratch_shapes=_scratch(1, tile),
        compiler_params=pltpu.CompilerParams(
            dimension_semantics=("arbitrary", "arbitrary"), vmem_limit_bytes=VMEM_LIMIT),
        name="prompt_layer",
    )(x, *w_args, *c_args)


def _sample_layer(x2d, states, w, l, n_steps, prev):
    batch = x2d.shape[0] // n_steps
    depth = states[0].shape[0]
    seqs = SAMPLE_SEQS
    assert batch % seqs == 0 and ROWS % n_steps == 0 and (seqs * n_steps) % CHUNK == 0 and CHUNK % n_steps == 0
    rows = seqs * n_steps

    def state_spec(n, step_major):
        if step_major:
            return pl.BlockSpec((1, n, seqs, GROUP), lambda i: (l, 0, i, 0))
        return pl.BlockSpec((1, seqs, n, GROUP), lambda i: (l, i, 0, 0))

    state_specs = [state_spec(n, sm) for n, _, sm in _STATES]
    w_specs, w_args = _weight_args(w, l, {"sp_w": "sp_w_sample", "sp_b": "sp_b_sample"})
    c_specs, c_args, aliases = _carried(prev, 1 + len(states) + len(w_args))
    return pl.pallas_call(
        functools.partial(_layer_kernel, sample=True, n_runs=seqs, seq_len=n_steps, n_carried=len(prev)),
        grid=(batch // seqs,),
        in_specs=[pl.BlockSpec((rows, D_MODEL), lambda i: (i, 0))] + state_specs + w_specs + c_specs,
        out_specs=[pl.BlockSpec((rows, D_MODEL), lambda i: (i, 0))] + state_specs
        + [pl.BlockSpec((1, rows, GROUP), lambda i: (l, i, 0))],
        out_shape=[jax.ShapeDtypeStruct(x2d.shape, x2d.dtype)]
        + [jax.ShapeDtypeStruct(s.shape, x2d.dtype) for s in states]
        + [jax.ShapeDtypeStruct((depth, batch * n_steps, GROUP), x2d.dtype)],
        input_output_aliases=aliases,
        scratch_shapes=_scratch(seqs, n_steps),
        compiler_params=pltpu.CompilerParams(
            dimension_semantics=("arbitrary",), vmem_limit_bytes=VMEM_LIMIT),
        name="sample_layer",
    )(x2d, *states, *w_args, *c_args)


def _prepare_weights(n_steps, pre_norm_g, w_in, conv_a_w, conv_a_b, ln_a_g, ln_a_b, conv_b_w, ln_c_g, ln_c_b,
                     spatial_w, spatial_b, pool_w, pool_scale, w_out, post_norm_g):
    row = lambda v: v[:, None, :]
    depth = w_in.shape[0]
    eye = jnp.eye(N_HEADS, dtype=pool_w.dtype)
    pool_bd = jnp.einsum("lgcd,gh->lgchd", pool_w, eye).reshape(depth, GROUP, GROUP)
    reps = CHUNK // n_steps
    corner = spatial_w[:, :, :n_steps, :n_steps]
    sw_sample = jnp.einsum("lhts,qr->lhqtrs", corner, jnp.eye(reps, dtype=corner.dtype)).reshape(
        depth, N_HEADS, CHUNK, CHUNK)
    sb_prompt = jnp.repeat(spatial_b.transpose(0, 2, 1), HEAD_DIM, axis=2)
    return {
        "pre_g": row(pre_norm_g), "w_in": w_in.astype(jnp.bfloat16),
        "ca_w": conv_a_w, "ca_b": row(conv_a_b), "lna_g": row(ln_a_g), "lna_b": row(ln_a_b),
        "cb_w": conv_b_w, "lnc_g": row(ln_c_g), "lnc_b": row(ln_c_b),
        "sp_w_prompt": spatial_w, "sp_b_prompt": sb_prompt,
        "sp_w_sample": sw_sample, "sp_b_sample": jnp.tile(sb_prompt[:, :n_steps], (1, reps, 1)),
        "pl_w": pool_bd.astype(jnp.bfloat16), "pl_s": row(pool_scale),
        "w_out": w_out.astype(jnp.bfloat16), "post_g": row(post_norm_g),
    }


def kernel(x_prompt, x_sample, state_conv_a, state_conv_b, state_pool, pre_norm_g, w_in, conv_a_w, conv_a_b, ln_a_g, ln_a_b, conv_b_w, ln_c_g, ln_c_b, spatial_w, spatial_b, pool_w, pool_scale, w_out, post_norm_g):
    depth = w_in.shape[0]
    dec_batch, n_steps, _ = x_sample.shape
    w = _prepare_weights(n_steps, pre_norm_g, w_in, conv_a_w, conv_a_b, ln_a_g, ln_a_b, conv_b_w, ln_c_g,
                         ln_c_b, spatial_w, spatial_b, pool_w, pool_scale, w_out, post_norm_g)
    step_major = lambda v, flag: jnp.swapaxes(v, 1, 2) if flag else v
    states = [step_major(s, sm) for s, (_, _, sm) in zip((state_conv_a, state_conv_b, state_pool), _STATES)]
    yp, ys = x_prompt, x_sample.reshape(dec_batch * n_steps, D_MODEL)
    prompt_out, sample_out = [], []
    for l in range(depth):
        yp, *prompt_out = _prompt_layer(yp, w, l, depth, prompt_out)
        ys, *sample_out = _sample_layer(ys, states, w, l, n_steps, sample_out)
    a_s, b_s, d_s = (step_major(s, sm) for s, (_, _, sm) in zip(sample_out[:3], _STATES))
    v_s = sample_out[3].reshape(depth, dec_batch, n_steps, GROUP)
    a_p, b_p, d_p = prompt_out
    return (yp, ys.reshape(x_sample.shape), a_p, a_s, b_p, b_s, d_p, d_s, v_s)
```

```python
import functools
import types

import jax
import jax.numpy as jnp
from jax import lax
from jax.experimental import pallas as pl
from jax.experimental.pallas import tpu as pltpu

D_MODEL = 1024
GROUP = 256
HEAD_DIM = 64
N_HEADS = 4
CONV_A_WIDTH = 31
CONV_B_WIDTH = 3
CHUNK = 128
POOL_BUF = 15
EPS = 1e-6
LANES = 128
BF16_ROWS = 16
HALVES = GROUP // LANES

(A_VAL, A_GATE, Z_A, B_B, B_C, B_X, Z_B, C_U, C_V, Z_C, D_X, Z_D) = range(12)
N_IN = 12 * GROUP
SPATIAL_OUT = C_V
POOL_OUT = D_X

PAST_A = 32
PAST_B = 8
PAST_D = 16

PROMPT_TILE = 1024
SAMPLE_SEQS = 64
ROWS = 32

VMEM_LIMIT = 58 * 1024 * 1024


def _sigmoid(v):
    return 1.0 / (1.0 + jnp.exp(-v))


def _silu(v):
    return v * _sigmoid(v)


def _rowsum(vs):
    tot = vs[0]
    for v in vs[1:]:
        tot = tot + v
    return jnp.sum(tot, axis=-1, keepdims=True)


def _half(h):
    return slice(h * LANES, (h + 1) * LANES)


def _slot(slot, h):
    return slice(slot * GROUP + h * LANES, slot * GROUP + (h + 1) * LANES)


def _layernorm2(v0, v1, g_ref, b_ref):
    mu = _rowsum([v0, v1]) * (1.0 / GROUP)
    d0, d1 = v0 - mu, v1 - mu
    var = _rowsum([d0 * d0, d1 * d1]) * (1.0 / GROUP)
    inv = lax.rsqrt(var + EPS)
    return (d0 * inv * g_ref[:, _half(0)] + b_ref[:, _half(0)],
            d1 * inv * g_ref[:, _half(1)] + b_ref[:, _half(1)])


def _copy_rows(dst, src, n_rows, chunk):
    def body(i, carry):
        r = pl.multiple_of(i * chunk, chunk)
        dst[pl.ds(r, chunk), :] = src[pl.ds(r, chunk), :].astype(dst.dtype)
        return carry
    lax.fori_loop(0, n_rows // chunk, body, 0)


class _History:
    def __init__(self, ref, past, seq_len):
        self.ref, self.past, self.seq_len = ref, past, seq_len
        self.piece = min(seq_len, ROWS)

    def _offsets(self, r0):
        for i in range(ROWS // self.piece):
            run, step = divmod(r0 + i * self.piece, self.seq_len)
            yield i, run * (self.past + self.seq_len) + self.past + step

    def window(self, h, r0, back, extra=0):
        assert extra == 0 or self.piece == ROWS
        parts = [self.ref[h, off - back - extra:off - back + self.piece, :] for _, off in self._offsets(r0)]
        return parts[0] if len(parts) == 1 else jnp.concatenate(parts, axis=0)

    def store(self, h, r0, value):
        for i, off in self._offsets(r0):
            self.ref[h, off:off + self.piece, :] = value[i * self.piece:(i + 1) * self.piece]

    def tail(self, h, run, n):
        end = (run + 1) * (self.past + self.seq_len)
        return self.ref[h, end - n:end, :]

    def set_history(self, h, run, value):
        start = run * (self.past + self.seq_len) + self.past
        self.ref[h, start - value.shape[0]:start, :] = value

    def _across_runs(self, row, n_runs):
        return pl.ds(row, n_runs, stride=self.past + self.seq_len)

    def tail_step(self, h, n_runs, j, n):
        return self.ref.at[h][self._across_runs(self.past + self.seq_len - n + j, n_runs), :]

    def set_history_step(self, h, j, n, value):
        self.ref.at[h][self._across_runs(self.past - n + j, value.shape[0]), :] = value


def _phase_in(t, p):
    for r0 in range(0, t.n_rows, ROWS):
        xb = t.x[r0:r0 + ROWS, :]
        ms = jnp.sum(xb * xb, axis=-1, keepdims=True) * (1.0 / D_MODEL)
        hb = xb * lax.rsqrt(ms + EPS) * p.pre_g[...]
        t.mix[r0:r0 + ROWS, :] = hb.astype(jnp.bfloat16)
    t.proj[...] = jnp.dot(t.mix[...], p.w_in[...], preferred_element_type=jnp.float32)


def _phase_mix(t, p):
    lo = lax.broadcasted_iota(jnp.int32, (ROWS, LANES), 1) < HEAD_DIM
    inv_w = [jnp.where(lo, 1.0 / 2, 1.0 / 4), jnp.where(lo, 1.0 / 8, 1.0 / 16)]
    win = [jnp.where(lo, 2, 4), jnp.where(lo, 8, 16)]
    hist_a, hist_b, hist_d = t.hist_a, t.hist_b, t.hist_d

    def pj(r0, slot, h):
        return t.proj[r0:r0 + ROWS, _slot(slot, h)]

    for r0 in range(0, t.n_rows, ROWS):
        for h in range(HALVES):
            hist_a.store(h, r0, pj(r0, A_VAL, h) * _sigmoid(pj(r0, A_GATE, h)))
            hist_b.store(h, r0, pj(r0, B_C, h) * pj(r0, B_X, h))
            hist_d.store(h, r0, pj(r0, D_X, h))
        v0, v1 = _layernorm2(pj(r0, C_V, 0), pj(r0, C_V, 1), p.lnc_g, p.lnc_b)
        t.vn[r0:r0 + ROWS, _half(0)] = v0
        t.vn[r0:r0 + ROWS, _half(1)] = v1

        a2 = hist_d.window(0, r0, 0) + hist_d.window(0, r0, 1)
        a4 = a2 + (hist_d.window(0, r0, 2) + hist_d.window(0, r0, 3))
        sums = [jnp.where(lo, a2, a4)]
        if hist_d.piece == ROWS:
            s8 = hist_d.window(1, r0, 0, extra=8)
            for j in range(1, 8):
                s8 = s8 + hist_d.window(1, r0, j, extra=8)
            s8_cur, s8_prev = s8[8:], s8[:ROWS]
        else:
            s8_cur, s8_prev = hist_d.window(1, r0, 0), hist_d.window(1, r0, 8)
            for j in range(1, 8):
                s8_cur = s8_cur + hist_d.window(1, r0, j)
                s8_prev = s8_prev + hist_d.window(1, r0, 8 + j)
        sums.append(jnp.where(lo, s8_cur, s8_cur + s8_prev))
        cnt = t.first_block_cnt() if r0 == 0 else None
        for h in range(HALVES):
            if cnt is None:
                mean = sums[h] * inv_w[h]
            else:
                mean = sums[h] / jnp.minimum(cnt, win[h]).astype(jnp.float32)
            t.mix[r0:r0 + ROWS, _slot(3, h)] = (mean - hist_d.window(h, r0, 0)).astype(jnp.bfloat16)

    lo_chunk = lax.broadcasted_iota(jnp.int32, (CHUNK, LANES), 1) < HEAD_DIM
    for c0 in range(0, t.n_rows, CHUNK):
        for h in range(HALVES):
            vb = t.vn[c0:c0 + CHUNK, _half(h)].astype(jnp.bfloat16)
            s_even = jnp.dot(p.sp_w[2 * h], vb, preferred_element_type=jnp.float32)
            s_odd = jnp.dot(p.sp_w[2 * h + 1], vb, preferred_element_type=jnp.float32)
            t.proj[c0:c0 + CHUNK, _slot(SPATIAL_OUT, h)] = (
                jnp.where(lo_chunk, s_even, s_odd) + t.chunk_bias(h))
    t.proj[:, POOL_OUT * GROUP:(POOL_OUT + 1) * GROUP] = jnp.dot(
        t.mix[:, 3 * GROUP:4 * GROUP], p.pl_w[...], preferred_element_type=jnp.float32)

    for r0 in range(0, t.n_rows, ROWS):
        _mix_block(t, p, r0)
    t.proj[:, 0:D_MODEL] = jnp.dot(t.mix[...], p.w_out[...], preferred_element_type=jnp.float32)


def _mix_block(t, p, r0):
    hist_a, hist_b = t.hist_a, t.hist_b

    def pj(r0, slot, h):
        return t.proj[r0:r0 + ROWS, _slot(slot, h)]

    ya = []
    for h in range(HALVES):
        acc = jnp.broadcast_to(p.ca_b[:, _half(h)], (ROWS, LANES))
        for k in range(CONV_A_WIDTH):
            acc = acc + p.ca_w[k:k + 1, _half(h)] * hist_a.window(h, r0, CONV_A_WIDTH - 1 - k)
        ya.append(acc)
    na = _layernorm2(ya[0], ya[1], p.lna_g, p.lna_b)
    for h in range(HALVES):
        t.mix[r0:r0 + ROWS, _slot(0, h)] = (_silu(na[h]) * _silu(pj(r0, Z_A, h))).astype(jnp.bfloat16)

    for h in range(HALVES):
        acc = None
        for k in range(CONV_B_WIDTH):
            term = p.cb_w[k:k + 1, _half(h)] * hist_b.window(h, r0, CONV_B_WIDTH - 1 - k)
            acc = term if acc is None else acc + term
        yb = pj(r0, B_B, h) * acc * _silu(pj(r0, Z_B, h))
        t.mix[r0:r0 + ROWS, _slot(1, h)] = yb.astype(jnp.bfloat16)

    for h in range(HALVES):
        yc = pj(r0, C_U, h) * pj(r0, SPATIAL_OUT, h) * _silu(pj(r0, Z_C, h))
        t.mix[r0:r0 + ROWS, _slot(2, h)] = yc.astype(jnp.bfloat16)
        yd = pj(r0, POOL_OUT, h) * p.pl_s[:, _half(h)] * _silu(pj(r0, Z_D, h))
        t.mix[r0:r0 + ROWS, _slot(3, h)] = yd.astype(jnp.bfloat16)


def _phase_out(t, p):
    for r0 in range(0, t.n_rows, ROWS):
        ob = t.proj[r0:r0 + ROWS, 0:D_MODEL]
        ms = jnp.sum(ob * ob, axis=-1, keepdims=True) * (1.0 / D_MODEL)
        t.y[r0:r0 + ROWS, :] = t.x[r0:r0 + ROWS, :] + ob * lax.rsqrt(ms + EPS) * p.post_g[...]


_WEIGHTS = ("pre_g", "w_in", "ca_w", "ca_b", "lna_g", "lna_b", "cb_w", "lnc_g", "lnc_b", "sp_w", "sp_b",
            "pl_w", "pl_s", "w_out", "post_g")
_STATES = ((CONV_A_WIDTH - 1, PAST_A, True), (CONV_B_WIDTH - 1, PAST_B, False), (POOL_BUF, PAST_D, True))


def _layer_kernel(*refs, sample, n_runs, seq_len, n_carried):
    n_state, n_w = len(_STATES), len(_WEIGHTS)
    n_in = 1 + (n_state if sample else 0) + n_w + n_carried
    n_out = 1 + n_state + (1 if sample else 0)
    x_ref, state_in = refs[0], refs[1:n_in - n_w - n_carried]
    w = dict(zip(_WEIGHTS, refs[n_in - n_w - n_carried:n_in - n_carried]))
    y_ref, state_out = refs[n_in], refs[n_in + 1:n_in + 1 + n_state]
    w_in, w_out, sp_w, ext_a, ext_b, ext_d, proj, vn_ref, mix = refs[n_in + n_out:]
    n_rows = n_runs * seq_len
    hists = [_History(ext, past, seq_len) for ext, (_, past, _) in zip((ext_a, ext_b, ext_d), _STATES)]

    first = pl.program_id(0) == 0
    if not sample:
        j = pl.program_id(1)
        first = first & (j == 0)

    @pl.when(first)
    def _():
        _copy_rows(w_in, w["w_in"].at[0], D_MODEL, BF16_ROWS)
        _copy_rows(w_out, w["w_out"].at[0], D_MODEL, BF16_ROWS)
        r_i = lax.broadcasted_iota(jnp.int32, (CHUNK, CHUNK), 0)
        c_i = lax.broadcasted_iota(jnp.int32, (CHUNK, CHUNK), 1)
        keep = c_i <= r_i
        if seq_len < CHUNK:
            keep = keep & (r_i // seq_len == c_i // seq_len)
            in_corner = lax.broadcasted_iota(jnp.int32, (seq_len, CHUNK), 1) < seq_len
        for hd in range(N_HEADS):
            mat = w["sp_w"][0, hd]
            if seq_len < CHUNK:
                corner = jnp.where(in_corner, mat[0:seq_len, :], 0.0)
                lanes = corner
                for q in range(1, CHUNK // seq_len):
                    lanes = lanes + pltpu.roll(corner, q * seq_len, axis=1)
                mat = jnp.concatenate([lanes] * (CHUNK // seq_len), axis=0)
            sp_w[hd] = jnp.where(keep, mat, 0.0).astype(jnp.bfloat16)

    if sample:
        for hist, ref, (n, _, step_major) in zip(hists, state_in, _STATES):
            for h in range(HALVES):
                if step_major:
                    for s in range(n):
                        hist.set_history_step(h, s, n, ref[0, s, :, _half(h)])
                else:
                    for run in range(n_runs):
                        hist.set_history(h, run, ref[0, run, :, _half(h)])
        first_block_cnt = lambda: None
    else:
        @pl.when(j == 0)
        def _():
            for hist in hists:
                hist.ref[:, 0:hist.past, :] = jnp.zeros((HALVES, hist.past, LANES), jnp.float32)

        @pl.when(j > 0)
        def _():
            for hist in hists:
                hist.ref[:, 0:hist.past, :] = hist.ref[:, seq_len:seq_len + hist.past, :]

        def first_block_cnt():
            return j * seq_len + lax.broadcasted_iota(jnp.int32, (ROWS, LANES), 0) + 1

    row = lambda name: w[name].at[0]
    p = types.SimpleNamespace(
        pre_g=row("pre_g"), w_in=w_in, ca_w=row("ca_w"), ca_b=row("ca_b"), lna_g=row("lna_g"),
        lna_b=row("lna_b"), cb_w=row("cb_w"), lnc_g=row("lnc_g"), lnc_b=row("lnc_b"), sp_w=sp_w,
        sp_b=row("sp_b"), pl_w=row("pl_w"), pl_s=row("pl_s"), w_out=w_out, post_g=row("post_g"))

    def chunk_bias(h):
        steps = min(seq_len, CHUNK)
        return jnp.concatenate([p.sp_b[0:steps, _half(h)]] * (CHUNK // steps), axis=0)

    t = types.SimpleNamespace(
        x=x_ref if sample else x_ref.at[0], y=y_ref if sample else y_ref.at[0],
        hist_a=hists[0], hist_b=hists[1], hist_d=hists[2], proj=proj, vn=vn_ref, mix=mix,
        n_rows=n_rows, first_block_cnt=first_block_cnt, chunk_bias=chunk_bias)
    _phase_in(t, p)
    _phase_mix(t, p)
    _phase_out(t, p)

    def write_states():
        for hist, ref, (n, _, step_major) in zip(hists, state_out, _STATES):
            for h in range(HALVES):
                if sample and step_major:
                    for s in range(n):
                        ref[0, s, :, _half(h)] = hist.tail_step(h, n_runs, s, n)
                else:
                    for run in range(n_runs):
                        ref[0, run, :, _half(h)] = hist.tail(h, run, n)

    if sample:
        write_states()
        refs[n_in + n_out - 1][0] = vn_ref[...]
    else:
        pl.when(j == pl.num_programs(1) - 1)(write_states)


def _layer_spec(arr, l, single_buffer=False):
    shape = (1,) + arr.shape[1:]
    index_map = lambda *_: (l,) + (0,) * (len(shape) - 1)
    if single_buffer:
        return pl.BlockSpec(shape, index_map, pipeline_mode=pl.Buffered(1))
    return pl.BlockSpec(shape, index_map)


def _scratch(n_runs, seq_len):
    n_rows = n_runs * seq_len
    return [
        pltpu.VMEM((D_MODEL, N_IN), jnp.bfloat16),
        pltpu.VMEM((D_MODEL, D_MODEL), jnp.bfloat16),
        pltpu.VMEM((N_HEADS, CHUNK, CHUNK), jnp.bfloat16),
    ] + [
        pltpu.VMEM((HALVES, n_runs * (past + seq_len), LANES), jnp.float32) for _, past, _ in _STATES
    ] + [
        pltpu.VMEM((n_rows, N_IN), jnp.float32),
        pltpu.VMEM((n_rows, GROUP), jnp.float32),
        pltpu.VMEM((n_rows, D_MODEL), jnp.bfloat16),
    ]


def _weight_args(w, l):
    specs = [_layer_spec(w[n], l, single_buffer=n in ("w_in", "w_out")) for n in _WEIGHTS]
    return specs, [w[n] for n in _WEIGHTS]


def _carried(prev, n_inputs):
    specs = [pl.BlockSpec(memory_space=pl.ANY) for _ in prev]
    return specs, list(prev), {n_inputs + k: 1 + k for k in range(len(prev))}


def _prompt_layer(x, w, l, depth, prev):
    batch, seq, _ = x.shape
    tile = PROMPT_TILE
    assert seq % tile == 0 and tile % CHUNK == 0 and tile >= PAST_A
    tok = pl.BlockSpec((1, tile, D_MODEL), lambda b, j: (b, j, 0))
    w_specs, w_args = _weight_args(w, l)
    c_specs, c_args, aliases = _carried(prev, 1 + len(w_args))
    return pl.pallas_call(
        functools.partial(_layer_kernel, sample=False, n_runs=1, seq_len=tile, n_carried=len(prev)),
        grid=(batch, seq // tile),
        in_specs=[tok] + w_specs + c_specs,
        out_specs=[tok] + [pl.BlockSpec((1, 1, n, GROUP), lambda b, j: (l, b, 0, 0)) for n, _, _ in _STATES],
        out_shape=[jax.ShapeDtypeStruct(x.shape, x.dtype)]
        + [jax.ShapeDtypeStruct((depth, batch, n, GROUP), x.dtype) for n, _, _ in _STATES],
        input_output_aliases=aliases,
        scratch_shapes=_scratch(1, tile),
        compiler_params=pltpu.CompilerParams(
            dimension_semantics=("arbitrary", "arbitrary"), vmem_limit_bytes=VMEM_LIMIT),
        name="prompt_layer",
    )(x, *w_args, *c_args)


def _sample_layer(x2d, states, w, l, n_steps, prev):
    batch = x2d.shape[0] // n_steps
    depth = states[0].shape[0]
    seqs = SAMPLE_SEQS
    assert batch % seqs == 0 and ROWS % n_steps == 0 and (seqs * n_steps) % CHUNK == 0 and CHUNK % n_steps == 0
    rows = seqs * n_steps

    def state_spec(n, step_major):
        if step_major:
            return pl.BlockSpec((1, n, seqs, GROUP), lambda i: (l, 0, i, 0))
        return pl.BlockSpec((1, seqs, n, GROUP), lambda i: (l, i, 0, 0))

    state_specs = [state_spec(n, sm) for n, _, sm in _STATES]
    w_specs, w_args = _weight_args(w, l)
    c_specs, c_args, aliases = _carried(prev, 1 + len(states) + len(w_args))
    return pl.pallas_call(
        functools.partial(_layer_kernel, sample=True, n_runs=seqs, seq_len=n_steps, n_carried=len(prev)),
        grid=(batch // seqs,),
        in_specs=[pl.BlockSpec((rows, D_MODEL), lambda i: (i, 0))] + state_specs + w_specs + c_specs,
        out_specs=[pl.BlockSpec((rows, D_MODEL), lambda i: (i, 0))] + state_specs
        + [pl.BlockSpec((1, rows, GROUP), lambda i: (l, i, 0))],
        out_shape=[jax.ShapeDtypeStruct(x2d.shape, x2d.dtype)]
        + [jax.ShapeDtypeStruct(s.shape, x2d.dtype) for s in states]
        + [jax.ShapeDtypeStruct((depth, batch * n_steps, GROUP), x2d.dtype)],
        input_output_aliases=aliases,
        scratch_shapes=_scratch(seqs, n_steps),
        compiler_params=pltpu.CompilerParams(
            dimension_semantics=("arbitrary",), vmem_limit_bytes=VMEM_LIMIT),
        name="sample_layer",
    )(x2d, *states, *w_args, *c_args)


def _prepare_weights(pre_norm_g, w_in, conv_a_w, conv_a_b, ln_a_g, ln_a_b, conv_b_w, ln_c_g, ln_c_b,
                     spatial_w, spatial_b, pool_w, pool_scale, w_out, post_norm_g):
    row = lambda v: v[:, None, :]
    depth = w_in.shape[0]
    eye = jnp.eye(N_HEADS, dtype=pool_w.dtype)
    pool_bd = jnp.einsum("lgcd,gh->lgchd", pool_w, eye).reshape(depth, GROUP, GROUP)
    return {
        "pre_g": row(pre_norm_g), "w_in": w_in.astype(jnp.bfloat16),
        "ca_w": conv_a_w, "ca_b": row(conv_a_b), "lna_g": row(ln_a_g), "lna_b": row(ln_a_b),
        "cb_w": conv_b_w, "lnc_g": row(ln_c_g), "lnc_b": row(ln_c_b),
        "sp_w": spatial_w, "sp_b": jnp.repeat(spatial_b.transpose(0, 2, 1), HEAD_DIM, axis=2),
        "pl_w": pool_bd.astype(jnp.bfloat16), "pl_s": row(pool_scale),
        "w_out": w_out.astype(jnp.bfloat16), "post_g": row(post_norm_g),
    }


def kernel(x_prompt, x_sample, state_conv_a, state_conv_b, state_pool, pre_norm_g, w_in, conv_a_w, conv_a_b, ln_a_g, ln_a_b, conv_b_w, ln_c_g, ln_c_b, spatial_w, spatial_b, pool_w, pool_scale, w_out, post_norm_g):
    depth = w_in.shape[0]
    dec_batch, n_steps, _ = x_sample.shape
    w = _prepare_weights(pre_norm_g, w_in, conv_a_w, conv_a_b, ln_a_g, ln_a_b, conv_b_w, ln_c_g,
                         ln_c_b, spatial_w, spatial_b, pool_w, pool_scale, w_out, post_norm_g)
    step_major = lambda v, flag: jnp.swapaxes(v, 1, 2) if flag else v
    states = [step_major(s, sm) for s, (_, _, sm) in zip((state_conv_a, state_conv_b, state_pool), _STATES)]
    yp, ys = x_prompt, x_sample.reshape(dec_batch * n_steps, D_MODEL)
    prompt_out, sample_out = [], []
    for l in range(depth):
        yp, *prompt_out = _prompt_layer(yp, w, l, depth, prompt_out)
        ys, *sample_out = _sample_layer(ys, states, w, l, n_steps, sample_out)
    a_s, b_s, d_s = (step_major(s, sm) for s, (_, _, sm) in zip(sample_out[:3], _STATES))
    v_s = sample_out[3].reshape(depth, dec_batch, n_steps, GROUP)
    a_p, b_p, d_p = prompt_out
    return (yp, ys.reshape(x_sample.shape), a_p, a_s, b_p, b_s, d_p, d_s, v_s)
```

```python
import functools
import types

import jax
import jax.numpy as jnp
from jax import lax
from jax.experimental import pallas as pl
from jax.experimental.pallas import tpu as pltpu

D_MODEL = 1024
GROUP = 256
HEAD_DIM = 64
N_HEADS = 4
CONV_A_WIDTH = 31
CONV_B_WIDTH = 3
CHUNK = 128
POOL_BUF = 15
EPS = 1e-6
LANES = 128
BF16_ROWS = 16
HALVES = GROUP // LANES

(A_VAL, A_GATE, Z_A, B_B, B_C, B_X, Z_B, C_U, C_V, Z_C, D_X, Z_D) = range(12)
N_IN = 12 * GROUP
SPATIAL_OUT = C_V
POOL_OUT = D_X

PAST_A = 32
PAST_B = 8
PAST_D = 16

PROMPT_TILE = 1024
SAMPLE_SEQS = 64
ROWS = 32

VMEM_LIMIT = 58 * 1024 * 1024


def _sigmoid(v):
    return 1.0 / (1.0 + jnp.exp(-v))


def _silu(v):
    return v * _sigmoid(v)


def _rowsum(vs):
    tot = vs[0]
    for v in vs[1:]:
        tot = tot + v
    return jnp.sum(tot, axis=-1, keepdims=True)


def _half(h):
    return slice(h * LANES, (h + 1) * LANES)


def _slot(slot, h):
    return slice(slot * GROUP + h * LANES, slot * GROUP + (h + 1) * LANES)


def _layernorm2(v0, v1, g_ref, b_ref):
    mu = _rowsum([v0, v1]) * (1.0 / GROUP)
    d0, d1 = v0 - mu, v1 - mu
    var = _rowsum([d0 * d0, d1 * d1]) * (1.0 / GROUP)
    inv = lax.rsqrt(var + EPS)
    return (d0 * inv * g_ref[:, _half(0)] + b_ref[:, _half(0)],
            d1 * inv * g_ref[:, _half(1)] + b_ref[:, _half(1)])


def _copy_rows(dst, src, n_rows, chunk):
    def body(i, carry):
        r = pl.multiple_of(i * chunk, chunk)
        dst[pl.ds(r, chunk), :] = src[pl.ds(r, chunk), :].astype(dst.dtype)
        return carry
    lax.fori_loop(0, n_rows // chunk, body, 0)


class _History:
    def __init__(self, ref, past, seq_len):
        self.ref, self.past, self.seq_len = ref, past, seq_len
        self.piece = min(seq_len, ROWS)

    def _offsets(self, r0):
        for i in range(ROWS // self.piece):
            run, step = divmod(r0 + i * self.piece, self.seq_len)
            yield i, run * (self.past + self.seq_len) + self.past + step

    def window(self, h, r0, back, extra=0):
        assert extra == 0 or self.piece == ROWS
        parts = [self.ref[h, off - back - extra:off - back + self.piece, :] for _, off in self._offsets(r0)]
        return parts[0] if len(parts) == 1 else jnp.concatenate(parts, axis=0)

    def store(self, h, r0, value):
        for i, off in self._offsets(r0):
            self.ref[h, off:off + self.piece, :] = value[i * self.piece:(i + 1) * self.piece]

    def tail(self, h, run, n):
        end = (run + 1) * (self.past + self.seq_len)
        return self.ref[h, end - n:end, :]

    def set_history(self, h, run, value):
        start = run * (self.past + self.seq_len) + self.past
        self.ref[h, start - value.shape[0]:start, :] = value

    def _across_runs(self, row, n_runs):
        return pl.ds(row, n_runs, stride=self.past + self.seq_len)

    def tail_step(self, h, n_runs, j, n):
        return self.ref.at[h][self._across_runs(self.past + self.seq_len - n + j, n_runs), :]

    def set_history_step(self, h, j, n, value):
        self.ref.at[h][self._across_runs(self.past - n + j, value.shape[0]), :] = value


def _phase_in(t, p):
    for r0 in range(0, t.n_rows, ROWS):
        xb = t.x[r0:r0 + ROWS, :]
        ms = jnp.sum(xb * xb, axis=-1, keepdims=True) * (1.0 / D_MODEL)
        hb = xb * lax.rsqrt(ms + EPS) * p.pre_g[...]
        t.mix[r0:r0 + ROWS, :] = hb.astype(jnp.bfloat16)
    t.proj[...] = jnp.dot(t.mix[...], p.w_in[...], preferred_element_type=jnp.float32)


def _phase_mix(t, p):
    lo = lax.broadcasted_iota(jnp.int32, (ROWS, LANES), 1) < HEAD_DIM
    inv_w = [jnp.where(lo, 1.0 / 2, 1.0 / 4), jnp.where(lo, 1.0 / 8, 1.0 / 16)]
    win = [jnp.where(lo, 2, 4), jnp.where(lo, 8, 16)]
    hist_a, hist_b, hist_d = t.hist_a, t.hist_b, t.hist_d

    def pj(r0, slot, h):
        return t.proj[r0:r0 + ROWS, _slot(slot, h)]

    for r0 in range(0, t.n_rows, ROWS):
        for h in range(HALVES):
            hist_a.store(h, r0, pj(r0, A_VAL, h) * _sigmoid(pj(r0, A_GATE, h)))
            hist_b.store(h, r0, pj(r0, B_C, h) * pj(r0, B_X, h))
            hist_d.store(h, r0, pj(r0, D_X, h))
        v0, v1 = _layernorm2(pj(r0, C_V, 0), pj(r0, C_V, 1), p.lnc_g, p.lnc_b)
        t.vn[r0:r0 + ROWS, _half(0)] = v0
        t.vn[r0:r0 + ROWS, _half(1)] = v1

        a2 = hist_d.window(0, r0, 0) + hist_d.window(0, r0, 1)
        a4 = a2 + (hist_d.window(0, r0, 2) + hist_d.window(0, r0, 3))
        sums = [jnp.where(lo, a2, a4)]
        if hist_d.piece == ROWS:
            s8 = hist_d.window(1, r0, 0, extra=8)
            for j in range(1, 8):
                s8 = s8 + hist_d.window(1, r0, j, extra=8)
            s8_cur, s8_prev = s8[8:], s8[:ROWS]
        else:
            s8_cur, s8_prev = hist_d.window(1, r0, 0), hist_d.window(1, r0, 8)
            for j in range(1, 8):
                s8_cur = s8_cur + hist_d.window(1, r0, j)
                s8_prev = s8_prev + hist_d.window(1, r0, 8 + j)
        sums.append(jnp.where(lo, s8_cur, s8_cur + s8_prev))
        cnt = t.first_block_cnt() if r0 == 0 else None
        for h in range(HALVES):
            if cnt is None:
                mean = sums[h] * inv_w[h]
            else:
                mean = sums[h] / jnp.minimum(cnt, win[h]).astype(jnp.float32)
            t.mix[r0:r0 + ROWS, _slot(3, h)] = (mean - hist_d.window(h, r0, 0)).astype(jnp.bfloat16)

    lo_chunk = lax.broadcasted_iota(jnp.int32, (CHUNK, LANES), 1) < HEAD_DIM
    for c0 in range(0, t.n_rows, CHUNK):
        for h in range(HALVES):
            vb = t.vn[c0:c0 + CHUNK, _half(h)].astype(jnp.bfloat16)
            s_even = jnp.dot(p.sp_w[2 * h], vb, preferred_element_type=jnp.float32)
            s_odd = jnp.dot(p.sp_w[2 * h + 1], vb, preferred_element_type=jnp.float32)
            t.proj[c0:c0 + CHUNK, _slot(SPATIAL_OUT, h)] = (
                jnp.where(lo_chunk, s_even, s_odd) + t.chunk_bias(h))
    t.proj[:, POOL_OUT * GROUP:(POOL_OUT + 1) * GROUP] = jnp.dot(
        t.mix[:, 3 * GROUP:4 * GROUP], p.pl_w[...], preferred_element_type=jnp.float32)

    for r0 in range(0, t.n_rows, ROWS):
        _mix_block(t, p, r0)
    t.proj[:, 0:D_MODEL] = jnp.dot(t.mix[...], p.w_out[...], preferred_element_type=jnp.float32)


def _mix_block(t, p, r0):
    hist_a, hist_b = t.hist_a, t.hist_b

    def pj(r0, slot, h):
        return t.proj[r0:r0 + ROWS, _slot(slot, h)]

    ya = []
    for h in range(HALVES):
        acc = jnp.broadcast_to(p.ca_b[:, _half(h)], (ROWS, LANES))
        for k in range(CONV_A_WIDTH):
            acc = acc + p.ca_w[k:k + 1, _half(h)] * hist_a.window(h, r0, CONV_A_WIDTH - 1 - k)
        ya.append(acc)
    na = _layernorm2(ya[0], ya[1], p.lna_g, p.lna_b)
    for h in range(HALVES):
        t.mix[r0:r0 + ROWS, _slot(0, h)] = (_silu(na[h]) * _silu(pj(r0, Z_A, h))).astype(jnp.bfloat16)

    for h in range(HALVES):
        acc = None
        for k in range(CONV_B_WIDTH):
            term = p.cb_w[k:k + 1, _half(h)] * hist_b.window(h, r0, CONV_B_WIDTH - 1 - k)
            acc = term if acc is None else acc + term
        yb = pj(r0, B_B, h) * acc * _silu(pj(r0, Z_B, h))
        t.mix[r0:r0 + ROWS, _slot(1, h)] = yb.astype(jnp.bfloat16)

    for h in range(HALVES):
        yc = pj(r0, C_U, h) * pj(r0, SPATIAL_OUT, h) * _silu(pj(r0, Z_C, h))
        t.mix[r0:r0 + ROWS, _slot(2, h)] = yc.astype(jnp.bfloat16)
        yd = pj(r0, POOL_OUT, h) * p.pl_s[:, _half(h)] * _silu(pj(r0, Z_D, h))
        t.mix[r0:r0 + ROWS, _slot(3, h)] = yd.astype(jnp.bfloat16)


def _phase_out(t, p):
    for r0 in range(0, t.n_rows, ROWS):
        ob = t.proj[r0:r0 + ROWS, 0:D_MODEL]
        ms = jnp.sum(ob * ob, axis=-1, keepdims=True) * (1.0 / D_MODEL)
        yb = t.x[r0:r0 + ROWS, :] + ob * lax.rsqrt(ms + EPS) * p.post_g[...]
        for y in t.ys:
            y[r0:r0 + ROWS, :] = yb


_WEIGHTS = ("pre_g", "w_in", "ca_w", "ca_b", "lna_g", "lna_b", "cb_w", "lnc_g", "lnc_b", "sp_w", "sp_b",
            "pl_w", "pl_s", "w_out", "post_g")
_STATES = ((CONV_A_WIDTH - 1, PAST_A, True), (CONV_B_WIDTH - 1, PAST_B, False), (POOL_BUF, PAST_D, True))


def _layer_kernel(*refs, sample, n_runs, seq_len, n_prev):
    n_state, n_w = len(_STATES), len(_WEIGHTS)
    n_stacked = n_state + (1 if sample else 0)
    n_carried = n_stacked if n_prev else 0
    n_in = 1 + (n_state if sample else 0) + n_w + n_carried
    n_out = 1 + n_stacked
    x_ref, state_in = refs[0], refs[1:n_in - n_w - n_carried]
    w = dict(zip(_WEIGHTS, refs[n_in - n_w - n_carried:n_in - n_carried]))
    carried = refs[n_in - n_carried:n_in]
    y_ref, stacked_out = refs[n_in], refs[n_in + 1:n_in + n_out]
    state_out = stacked_out[:n_state]
    w_in, w_out, sp_w, ext_a, ext_b, ext_d, proj, vn_ref, mix = refs[n_in + n_out:n_in + n_out + 9]
    n_rows = n_runs * seq_len
    hists = [_History(ext, past, seq_len) for ext, (_, past, _) in zip((ext_a, ext_b, ext_d), _STATES)]

    if sample:
        first = pl.program_id(1) == 0
        tokens = refs[n_in + n_out + 9].at[pl.program_id(1)]

        @pl.when(pl.program_id(0) == 0)
        def _():
            tokens[...] = x_ref[...]
    else:
        j = pl.program_id(1)
        first = (pl.program_id(0) == 0) & (j == 0)

    @pl.when(first)
    def _():
        _copy_rows(w_in, w["w_in"].at[0], D_MODEL, BF16_ROWS)
        _copy_rows(w_out, w["w_out"].at[0], D_MODEL, BF16_ROWS)
        r_i = lax.broadcasted_iota(jnp.int32, (CHUNK, CHUNK), 0)
        c_i = lax.broadcasted_iota(jnp.int32, (CHUNK, CHUNK), 1)
        keep = c_i <= r_i
        if seq_len < CHUNK:
            keep = keep & (r_i // seq_len == c_i // seq_len)
            in_corner = lax.broadcasted_iota(jnp.int32, (seq_len, CHUNK), 1) < seq_len
        for hd in range(N_HEADS):
            mat = w["sp_w"][0, hd]
            if seq_len < CHUNK:
                corner = jnp.where(in_corner, mat[0:seq_len, :], 0.0)
                lanes = corner
                for q in range(1, CHUNK // seq_len):
                    lanes = lanes + pltpu.roll(corner, q * seq_len, axis=1)
                mat = jnp.concatenate([lanes] * (CHUNK // seq_len), axis=0)
            sp_w[hd] = jnp.where(keep, mat, 0.0).astype(jnp.bfloat16)

    if sample:
        for hist, ref, (n, _, step_major) in zip(hists, state_in, _STATES):
            for h in range(HALVES):
                if step_major:
                    for s in range(n):
                        hist.set_history_step(h, s, n, ref[0, s, :, _half(h)])
                else:
                    for run in range(n_runs):
                        hist.set_history(h, run, ref[0, run, :, _half(h)])
        first_block_cnt = lambda: None
    else:
        @pl.when(j == 0)
        def _():
            for hist in hists:
                hist.ref[:, 0:hist.past, :] = jnp.zeros((HALVES, hist.past, LANES), jnp.float32)

        @pl.when(j > 0)
        def _():
            for hist in hists:
                hist.ref[:, 0:hist.past, :] = hist.ref[:, seq_len:seq_len + hist.past, :]

        def first_block_cnt():
            return j * seq_len + lax.broadcasted_iota(jnp.int32, (ROWS, LANES), 0) + 1

    row = lambda name: w[name].at[0]
    p = types.SimpleNamespace(
        pre_g=row("pre_g"), w_in=w_in, ca_w=row("ca_w"), ca_b=row("ca_b"), lna_g=row("lna_g"),
        lna_b=row("lna_b"), cb_w=row("cb_w"), lnc_g=row("lnc_g"), lnc_b=row("lnc_b"), sp_w=sp_w,
        sp_b=row("sp_b"), pl_w=row("pl_w"), pl_s=row("pl_s"), w_out=w_out, post_g=row("post_g"))

    def chunk_bias(h):
        steps = min(seq_len, CHUNK)
        return jnp.concatenate([p.sp_b[0:steps, _half(h)]] * (CHUNK // steps), axis=0)

    t = types.SimpleNamespace(
        x=tokens if sample else x_ref.at[0], ys=(tokens, y_ref) if sample else (y_ref.at[0],),
        hist_a=hists[0], hist_b=hists[1], hist_d=hists[2], proj=proj, vn=vn_ref, mix=mix,
        n_rows=n_rows, first_block_cnt=first_block_cnt, chunk_bias=chunk_bias)
    _phase_in(t, p)
    _phase_mix(t, p)
    _phase_out(t, p)

    def write_stacked():
        for ref, src in zip(stacked_out, carried):
            for k in range(n_prev):
                ref[k] = src[k]
        for hist, ref, (n, _, step_major) in zip(hists, state_out, _STATES):
            for h in range(HALVES):
                if sample and step_major:
                    for s in range(n):
                        ref[n_prev, s, :, _half(h)] = hist.tail_step(h, n_runs, s, n)
                else:
                    for run in range(n_runs):
                        ref[n_prev, run, :, _half(h)] = hist.tail(h, run, n)
        if sample:
            stacked_out[n_state][n_prev] = vn_ref[...]

    if sample:
        write_stacked()
    else:
        pl.when(j == pl.num_programs(1) - 1)(write_stacked)


def _layer_spec(arr, l, single_buffer=False):
    shape = (1,) + arr.shape[1:]
    index_map = lambda *g: (g[0] if l is None else l,) + (0,) * (len(shape) - 1)
    if single_buffer:
        return pl.BlockSpec(shape, index_map, pipeline_mode=pl.Buffered(1))
    return pl.BlockSpec(shape, index_map)


def _scratch(n_runs, seq_len):
    n_rows = n_runs * seq_len
    return [
        pltpu.VMEM((D_MODEL, N_IN), jnp.bfloat16),
        pltpu.VMEM((D_MODEL, D_MODEL), jnp.bfloat16),
        pltpu.VMEM((N_HEADS, CHUNK, CHUNK), jnp.bfloat16),
    ] + [
        pltpu.VMEM((HALVES, n_runs * (past + seq_len), LANES), jnp.float32) for _, past, _ in _STATES
    ] + [
        pltpu.VMEM((n_rows, N_IN), jnp.float32),
        pltpu.VMEM((n_rows, GROUP), jnp.float32),
        pltpu.VMEM((n_rows, D_MODEL), jnp.bfloat16),
    ]


def _weight_args(w, l):
    specs = [_layer_spec(w[n], l, single_buffer=n in ("w_in", "w_out")) for n in _WEIGHTS]
    return specs, [w[n] for n in _WEIGHTS]


def _stacked_specs(n_layers, tails):
    def spec(block_tail, tail_index):
        return pl.BlockSpec((n_layers,) + block_tail, lambda *g: (0,) + tail_index(*g))

    return [spec(*tail) for tail in tails]


def _prompt_layer(x, w, l, prev):
    batch, seq, _ = x.shape
    tile = PROMPT_TILE
    assert seq % tile == 0 and tile % CHUNK == 0 and tile >= PAST_A
    tok = pl.BlockSpec((1, tile, D_MODEL), lambda b, j: (b, j, 0))
    tails = [((1, n, GROUP), lambda b, j: (b, 0, 0)) for n, _, _ in _STATES]
    w_specs, w_args = _weight_args(w, l)
    return pl.pallas_call(
        functools.partial(_layer_kernel, sample=False, n_runs=1, seq_len=tile, n_prev=l),
        grid=(batch, seq // tile),
        in_specs=[tok] + w_specs + (_stacked_specs(l, tails) if prev else []),
        out_specs=[tok] + _stacked_specs(l + 1, tails),
        out_shape=[jax.ShapeDtypeStruct(x.shape, x.dtype)]
        + [jax.ShapeDtypeStruct((l + 1, batch, n, GROUP), x.dtype) for n, _, _ in _STATES],
        scratch_shapes=_scratch(1, tile),
        compiler_params=pltpu.CompilerParams(
            dimension_semantics=("arbitrary", "arbitrary"), vmem_limit_bytes=VMEM_LIMIT),
        name="prompt_layer",
    )(x, *w_args, *prev)


def _sample_layers(x2d, states, w, n_steps):
    batch = x2d.shape[0] // n_steps
    depth = states[0].shape[0]
    seqs = SAMPLE_SEQS
    assert batch % seqs == 0 and ROWS % n_steps == 0 and (seqs * n_steps) % CHUNK == 0 and CHUNK % n_steps == 0
    rows = seqs * n_steps
    state_specs = [pl.BlockSpec((1, n, seqs, GROUP), lambda l, i: (l, 0, i, 0)) if step_major
                   else pl.BlockSpec((1, seqs, n, GROUP), lambda l, i: (l, i, 0, 0))
                   for n, _, step_major in _STATES]
    tok = pl.BlockSpec((rows, D_MODEL), lambda l, i: (i, 0))
    w_specs, w_args = _weight_args(w, None)
    return pl.pallas_call(
        functools.partial(_layer_kernel, sample=True, n_runs=seqs, seq_len=n_steps, n_prev=0),
        grid=(depth, batch // seqs),
        in_specs=[tok] + state_specs + w_specs,
        out_specs=[tok] + state_specs + [pl.BlockSpec((1, rows, GROUP), lambda l, i: (l, i, 0))],
        out_shape=[jax.ShapeDtypeStruct(x2d.shape, x2d.dtype)]
        + [jax.ShapeDtypeStruct(s.shape, x2d.dtype) for s in states]
        + [jax.ShapeDtypeStruct((depth, batch * n_steps, GROUP), x2d.dtype)],
        scratch_shapes=_scratch(seqs, n_steps) + [pltpu.VMEM((batch // seqs, rows, D_MODEL), jnp.float32)],
        compiler_params=pltpu.CompilerParams(
            dimension_semantics=("arbitrary", "arbitrary"), vmem_limit_bytes=VMEM_LIMIT),
        name="sample_layers",
    )(x2d, *states, *w_args)


def _prepare_weights(pre_norm_g, w_in, conv_a_w, conv_a_b, ln_a_g, ln_a_b, conv_b_w, ln_c_g, ln_c_b,
                     spatial_w, spatial_b, pool_w, pool_scale, w_out, post_norm_g):
    row = lambda v: v[:, None, :]
    depth = w_in.shape[0]
    eye = jnp.eye(N_HEADS, dtype=pool_w.dtype)
    pool_bd = jnp.einsum("lgcd,gh->lgchd", pool_w, eye).reshape(depth, GROUP, GROUP)
    return {
        "pre_g": row(pre_norm_g), "w_in": w_in.astype(jnp.bfloat16),
        "ca_w": conv_a_w, "ca_b": row(conv_a_b), "lna_g": row(ln_a_g), "lna_b": row(ln_a_b),
        "cb_w": conv_b_w, "lnc_g": row(ln_c_g), "lnc_b": row(ln_c_b),
        "sp_w": spatial_w, "sp_b": jnp.repeat(spatial_b.transpose(0, 2, 1), HEAD_DIM, axis=2),
        "pl_w": pool_bd.astype(jnp.bfloat16), "pl_s": row(pool_scale),
        "w_out": w_out.astype(jnp.bfloat16), "post_g": row(post_norm_g),
    }


def kernel(x_prompt, x_sample, state_conv_a, state_conv_b, state_pool, pre_norm_g, w_in, conv_a_w, conv_a_b, ln_a_g, ln_a_b, conv_b_w, ln_c_g, ln_c_b, spatial_w, spatial_b, pool_w, pool_scale, w_out, post_norm_g):
    depth = w_in.shape[0]
    dec_batch, n_steps, _ = x_sample.shape
    w = _prepare_weights(pre_norm_g, w_in, conv_a_w, conv_a_b, ln_a_g, ln_a_b, conv_b_w, ln_c_g,
                         ln_c_b, spatial_w, spatial_b, pool_w, pool_scale, w_out, post_norm_g)
    step_major = lambda v, flag: jnp.swapaxes(v, 1, 2) if flag else v
    states = [step_major(s, sm) for s, (_, _, sm) in zip((state_conv_a, state_conv_b, state_pool), _STATES)]
    yp, ys = x_prompt, x_sample.reshape(dec_batch * n_steps, D_MODEL)
    prompt_out = []
    for l in range(depth):
        yp, *prompt_out = _prompt_layer(yp, w, l, prompt_out)
    ys, *sample_out = _sample_layers(ys, states, w, n_steps)
    a_s, b_s, d_s = (step_major(s, sm) for s, (_, _, sm) in zip(sample_out[:3], _STATES))
    v_s = sample_out[3].reshape(depth, dec_batch, n_steps, GROUP)
    a_p, b_p, d_p = prompt_out
    return (yp, ys.reshape(x_sample.shape), a_p, a_s, b_p, b_s, d_p, d_s, v_s)
```

```python
import functools
import types

import jax
import jax.numpy as jnp
from jax import lax
from jax.experimental import pallas as pl
from jax.experimental.pallas import tpu as pltpu

D_MODEL = 1024
GROUP = 256
HEAD_DIM = 64
N_HEADS = 4
CONV_A_WIDTH = 31
CONV_B_WIDTH = 3
CHUNK = 128
POOL_BUF = 15
EPS = 1e-6
LANES = 128
HALVES = GROUP // LANES

(A_VAL, A_GATE, Z_A, B_B, B_C, B_X, Z_B, C_U, C_V, Z_C, D_X, Z_D) = range(12)
N_IN = 12 * GROUP
SPATIAL_OUT = C_V
POOL_OUT = D_X

PAST_A = 32
PAST_B = 8
PAST_D = 16

PROMPT_TILE = 1024
SAMPLE_SEQS = 64
ROWS = 32
WEIGHT_CHUNK = 128

VMEM_LIMIT = 58 * 1024 * 1024


def _sigmoid(v):
    return 1.0 / (1.0 + jnp.exp(-v))


def _silu(v):
    return v * _sigmoid(v)


def _rowsum(vs):
    tot = vs[0]
    for v in vs[1:]:
        tot = tot + v
    return jnp.sum(tot, axis=-1, keepdims=True)


def _half(h):
    return slice(h * LANES, (h + 1) * LANES)


def _slot(slot, h):
    return slice(slot * GROUP + h * LANES, slot * GROUP + (h + 1) * LANES)


def _layernorm2(v0, v1, g_ref, b_ref):
    mu = _rowsum([v0, v1]) * (1.0 / GROUP)
    d0, d1 = v0 - mu, v1 - mu
    var = _rowsum([d0 * d0, d1 * d1]) * (1.0 / GROUP)
    inv = lax.rsqrt(var + EPS)
    return (d0 * inv * g_ref[:, _half(0)] + b_ref[:, _half(0)],
            d1 * inv * g_ref[:, _half(1)] + b_ref[:, _half(1)])


def _load_weight(hbm, layer, stage, sem, dst):
    n_rows = dst.shape[0]
    chunk = stage.shape[1]
    n_chunks = n_rows // chunk

    def copy(c):
        return pltpu.make_async_copy(hbm.at[layer, pl.ds(c * chunk, chunk), :], stage.at[c % 2], sem.at[c % 2])

    copy(0).start()
    for c in range(n_chunks):
        if c + 1 < n_chunks:
            copy(c + 1).start()
        copy(c).wait()
        dst[c * chunk:(c + 1) * chunk, :] = stage[c % 2].astype(dst.dtype)


class _History:
    def __init__(self, ref, past, seq_len):
        self.ref, self.past, self.seq_len = ref, past, seq_len
        self.piece = min(seq_len, ROWS)

    def _offsets(self, r0):
        for i in range(ROWS // self.piece):
            run, step = divmod(r0 + i * self.piece, self.seq_len)
            yield i, run * (self.past + self.seq_len) + self.past + step

    def window(self, h, r0, back, extra=0):
        assert extra == 0 or self.piece == ROWS
        parts = [self.ref[h, off - back - extra:off - back + self.piece, :] for _, off in self._offsets(r0)]
        return parts[0] if len(parts) == 1 else jnp.concatenate(parts, axis=0)

    def store(self, h, r0, value):
        for i, off in self._offsets(r0):
            self.ref[h, off:off + self.piece, :] = value[i * self.piece:(i + 1) * self.piece]

    def tail(self, h, run, n):
        end = (run + 1) * (self.past + self.seq_len)
        return self.ref[h, end - n:end, :]

    def set_history(self, h, run, value):
        start = run * (self.past + self.seq_len) + self.past
        self.ref[h, start - value.shape[0]:start, :] = value

    def _across_runs(self, row, n_runs):
        return pl.ds(row, n_runs, stride=self.past + self.seq_len)

    def tail_step(self, h, n_runs, j, n):
        return self.ref.at[h][self._across_runs(self.past + self.seq_len - n + j, n_runs), :]

    def set_history_step(self, h, j, n, value):
        self.ref.at[h][self._across_runs(self.past - n + j, value.shape[0]), :] = value


def _phase_in(t, p):
    for r0 in range(0, t.n_rows, ROWS):
        xb = t.x[r0:r0 + ROWS, :]
        ms = jnp.sum(xb * xb, axis=-1, keepdims=True) * (1.0 / D_MODEL)
        hb = xb * lax.rsqrt(ms + EPS) * p.pre_g[...]
        t.mix[r0:r0 + ROWS, :] = hb.astype(jnp.bfloat16)
    t.proj[...] = jnp.dot(t.mix[...], p.w_in[...], preferred_element_type=jnp.float32)


def _phase_mix(t, p):
    lo = lax.broadcasted_iota(jnp.int32, (ROWS, LANES), 1) < HEAD_DIM
    inv_w = [jnp.where(lo, 1.0 / 2, 1.0 / 4), jnp.where(lo, 1.0 / 8, 1.0 / 16)]
    win = [jnp.where(lo, 2, 4), jnp.where(lo, 8, 16)]
    hist_a, hist_b, hist_d = t.hist_a, t.hist_b, t.hist_d

    def pj(r0, slot, h):
        return t.proj[r0:r0 + ROWS, _slot(slot, h)]

    for r0 in range(0, t.n_rows, ROWS):
        for h in range(HALVES):
            hist_a.store(h, r0, pj(r0, A_VAL, h) * _sigmoid(pj(r0, A_GATE, h)))
            hist_b.store(h, r0, pj(r0, B_C, h) * pj(r0, B_X, h))
            hist_d.store(h, r0, pj(r0, D_X, h))
        v0, v1 = _layernorm2(pj(r0, C_V, 0), pj(r0, C_V, 1), p.lnc_g, p.lnc_b)
        t.vn[r0:r0 + ROWS, _half(0)] = v0
        t.vn[r0:r0 + ROWS, _half(1)] = v1

        a2 = hist_d.window(0, r0, 0) + hist_d.window(0, r0, 1)
        a4 = a2 + (hist_d.window(0, r0, 2) + hist_d.window(0, r0, 3))
        sums = [jnp.where(lo, a2, a4)]
        if hist_d.piece == ROWS:
            s8 = hist_d.window(1, r0, 0, extra=8)
            for j in range(1, 8):
                s8 = s8 + hist_d.window(1, r0, j, extra=8)
            s8_cur, s8_prev = s8[8:], s8[:ROWS]
        else:
            s8_cur, s8_prev = hist_d.window(1, r0, 0), hist_d.window(1, r0, 8)
            for j in range(1, 8):
                s8_cur = s8_cur + hist_d.window(1, r0, j)
                s8_prev = s8_prev + hist_d.window(1, r0, 8 + j)
        sums.append(jnp.where(lo, s8_cur, s8_cur + s8_prev))
        cnt = t.first_block_cnt() if r0 == 0 else None
        for h in range(HALVES):
            if cnt is None:
                mean = sums[h] * inv_w[h]
            else:
                mean = sums[h] / jnp.minimum(cnt, win[h]).astype(jnp.float32)
            t.mix[r0:r0 + ROWS, _slot(3, h)] = (mean - hist_d.window(h, r0, 0)).astype(jnp.bfloat16)

    lo_chunk = lax.broadcasted_iota(jnp.int32, (CHUNK, LANES), 1) < HEAD_DIM
    for c0 in range(0, t.n_rows, CHUNK):
        for h in range(HALVES):
            vb = t.vn[c0:c0 + CHUNK, _half(h)].astype(jnp.bfloat16)
            s_even = jnp.dot(p.sp_w[2 * h], vb, preferred_element_type=jnp.float32)
            s_odd = jnp.dot(p.sp_w[2 * h + 1], vb, preferred_element_type=jnp.float32)
            t.proj[c0:c0 + CHUNK, _slot(SPATIAL_OUT, h)] = (
                jnp.where(lo_chunk, s_even, s_odd) + t.chunk_bias(h))
    t.proj[:, POOL_OUT * GROUP:(POOL_OUT + 1) * GROUP] = jnp.dot(
        t.mix[:, 3 * GROUP:4 * GROUP], p.pl_w[...], preferred_element_type=jnp.float32)

    for r0 in range(0, t.n_rows, ROWS):
        _mix_block(t, p, r0)
    t.proj[:, 0:D_MODEL] = jnp.dot(t.mix[...], p.w_out[...], preferred_element_type=jnp.float32)


def _mix_block(t, p, r0):
    hist_a, hist_b = t.hist_a, t.hist_b

    def pj(r0, slot, h):
        return t.proj[r0:r0 + ROWS, _slot(slot, h)]

    ya = []
    for h in range(HALVES):
        acc = jnp.broadcast_to(p.ca_b[:, _half(h)], (ROWS, LANES))
        for k in range(CONV_A_WIDTH):
            acc = acc + p.ca_w[k:k + 1, _half(h)] * hist_a.window(h, r0, CONV_A_WIDTH - 1 - k)
        ya.append(acc)
    na = _layernorm2(ya[0], ya[1], p.lna_g, p.lna_b)
    for h in range(HALVES):
        t.mix[r0:r0 + ROWS, _slot(0, h)] = (_silu(na[h]) * _silu(pj(r0, Z_A, h))).astype(jnp.bfloat16)

    for h in range(HALVES):
        acc = None
        for k in range(CONV_B_WIDTH):
            term = p.cb_w[k:k + 1, _half(h)] * hist_b.window(h, r0, CONV_B_WIDTH - 1 - k)
            acc = term if acc is None else acc + term
        yb = pj(r0, B_B, h) * acc * _silu(pj(r0, Z_B, h))
        t.mix[r0:r0 + ROWS, _slot(1, h)] = yb.astype(jnp.bfloat16)

    for h in range(HALVES):
        yc = pj(r0, C_U, h) * pj(r0, SPATIAL_OUT, h) * _silu(pj(r0, Z_C, h))
        t.mix[r0:r0 + ROWS, _slot(2, h)] = yc.astype(jnp.bfloat16)
        yd = pj(r0, POOL_OUT, h) * p.pl_s[:, _half(h)] * _silu(pj(r0, Z_D, h))
        t.mix[r0:r0 + ROWS, _slot(3, h)] = yd.astype(jnp.bfloat16)


def _phase_out(t, p):
    for r0 in range(0, t.n_rows, ROWS):
        ob = t.proj[r0:r0 + ROWS, 0:D_MODEL]
        ms = jnp.sum(ob * ob, axis=-1, keepdims=True) * (1.0 / D_MODEL)
        yb = t.x[r0:r0 + ROWS, :] + ob * lax.rsqrt(ms + EPS) * p.post_g[...]
        for y in t.ys:
            y[r0:r0 + ROWS, :] = yb


_WEIGHTS = ("pre_g", "w_in", "ca_w", "ca_b", "lna_g", "lna_b", "cb_w", "lnc_g", "lnc_b", "sp_w", "sp_b",
            "pl_w", "pl_s", "w_out", "post_g")
_STATES = ((CONV_A_WIDTH - 1, PAST_A, True), (CONV_B_WIDTH - 1, PAST_B, False), (POOL_BUF, PAST_D, True))


def _layer_kernel(*refs, sample, n_runs, seq_len, n_prev):
    n_state, n_w = len(_STATES), len(_WEIGHTS)
    n_stacked = n_state + (1 if sample else 0)
    n_carried = n_stacked if n_prev else 0
    n_in = 1 + (n_state if sample else 0) + n_w + n_carried
    n_out = 1 + n_stacked
    x_ref, state_in = refs[0], refs[1:n_in - n_w - n_carried]
    w = dict(zip(_WEIGHTS, refs[n_in - n_w - n_carried:n_in - n_carried]))
    carried = refs[n_in - n_carried:n_in]
    y_ref, stacked_out = refs[n_in], refs[n_in + 1:n_in + n_out]
    state_out = stacked_out[:n_state]
    scratch = refs[n_in + n_out:]
    w_in, w_out, sp_w, stage_in, stage_out, sems, ext_a, ext_b, ext_d, proj, vn_ref, mix = scratch[:12]
    n_rows = n_runs * seq_len
    hists = [_History(ext, past, seq_len) for ext, (_, past, _) in zip((ext_a, ext_b, ext_d), _STATES)]

    if sample:
        layer = pl.program_id(0)
        first = pl.program_id(1) == 0
        tokens = scratch[12].at[pl.program_id(1)]

        @pl.when(layer == 0)
        def _():
            tokens[...] = x_ref[...]
    else:
        layer = n_prev
        j = pl.program_id(1)
        first = (pl.program_id(0) == 0) & (j == 0)

    @pl.when(first)
    def _():
        _load_weight(w["w_in"], layer, stage_in, sems.at[0], w_in)
        _load_weight(w["w_out"], layer, stage_out, sems.at[1], w_out)
        r_i = lax.broadcasted_iota(jnp.int32, (CHUNK, CHUNK), 0)
        c_i = lax.broadcasted_iota(jnp.int32, (CHUNK, CHUNK), 1)
        keep = c_i <= r_i
        if seq_len < CHUNK:
            keep = keep & (r_i // seq_len == c_i // seq_len)
            in_corner = lax.broadcasted_iota(jnp.int32, (seq_len, CHUNK), 1) < seq_len
        for hd in range(N_HEADS):
            mat = w["sp_w"][0, hd]
            if seq_len < CHUNK:
                corner = jnp.where(in_corner, mat[0:seq_len, :], 0.0)
                lanes = corner
                for q in range(1, CHUNK // seq_len):
                    lanes = lanes + pltpu.roll(corner, q * seq_len, axis=1)
                mat = jnp.concatenate([lanes] * (CHUNK // seq_len), axis=0)
            sp_w[hd] = jnp.where(keep, mat, 0.0).astype(jnp.bfloat16)

    if sample:
        for hist, ref, (n, _, step_major) in zip(hists, state_in, _STATES):
            for h in range(HALVES):
                if step_major:
                    for s in range(n):
                        hist.set_history_step(h, s, n, ref[0, s, :, _half(h)])
                else:
                    for run in range(n_runs):
                        hist.set_history(h, run, ref[0, run, :, _half(h)])
        first_block_cnt = lambda: None
    else:
        @pl.when(j == 0)
        def _():
            for hist in hists:
                hist.ref[:, 0:hist.past, :] = jnp.zeros((HALVES, hist.past, LANES), jnp.float32)

        @pl.when(j > 0)
        def _():
            for hist in hists:
                hist.ref[:, 0:hist.past, :] = hist.ref[:, seq_len:seq_len + hist.past, :]

        def first_block_cnt():
            return j * seq_len + lax.broadcasted_iota(jnp.int32, (ROWS, LANES), 0) + 1

    row = lambda name: w[name].at[0]
    p = types.SimpleNamespace(
        pre_g=row("pre_g"), w_in=w_in, ca_w=row("ca_w"), ca_b=row("ca_b"), lna_g=row("lna_g"),
        lna_b=row("lna_b"), cb_w=row("cb_w"), lnc_g=row("lnc_g"), lnc_b=row("lnc_b"), sp_w=sp_w,
        sp_b=row("sp_b"), pl_w=row("pl_w"), pl_s=row("pl_s"), w_out=w_out, post_g=row("post_g"))

    def chunk_bias(h):
        steps = min(seq_len, CHUNK)
        return jnp.concatenate([p.sp_b[0:steps, _half(h)]] * (CHUNK // steps), axis=0)

    t = types.SimpleNamespace(
        x=tokens if sample else x_ref.at[0], ys=(tokens, y_ref) if sample else (y_ref.at[0],),
        hist_a=hists[0], hist_b=hists[1], hist_d=hists[2], proj=proj, vn=vn_ref, mix=mix,
        n_rows=n_rows, first_block_cnt=first_block_cnt, chunk_bias=chunk_bias)
    _phase_in(t, p)
    _phase_mix(t, p)
    _phase_out(t, p)

    def write_stacked():
        for ref, src in zip(stacked_out, carried):
            for k in range(n_prev):
                ref[k] = src[k]
        for hist, ref, (n, _, step_major) in zip(hists, state_out, _STATES):
            for h in range(HALVES):
                if sample and step_major:
                    for s in range(n):
                        ref[n_prev, s, :, _half(h)] = hist.tail_step(h, n_runs, s, n)
                else:
                    for run in range(n_runs):
                        ref[n_prev, run, :, _half(h)] = hist.tail(h, run, n)
        if sample:
            stacked_out[n_state][n_prev] = vn_ref[...]

    if sample:
        write_stacked()
    else:
        pl.when(j == pl.num_programs(1) - 1)(write_stacked)


def _layer_spec(arr, l):
    shape = (1,) + arr.shape[1:]
    return pl.BlockSpec(shape, lambda *g: (g[0] if l is None else l,) + (0,) * (len(shape) - 1))


def _scratch(n_runs, seq_len):
    n_rows = n_runs * seq_len
    return [
        pltpu.VMEM((D_MODEL, N_IN), jnp.bfloat16),
        pltpu.VMEM((D_MODEL, D_MODEL), jnp.bfloat16),
        pltpu.VMEM((N_HEADS, CHUNK, CHUNK), jnp.bfloat16),
        pltpu.VMEM((2, WEIGHT_CHUNK, N_IN), jnp.float32),
        pltpu.VMEM((2, WEIGHT_CHUNK, D_MODEL), jnp.float32),
        pltpu.SemaphoreType.DMA((2, 2)),
    ] + [
        pltpu.VMEM((HALVES, n_runs * (past + seq_len), LANES), jnp.float32) for _, past, _ in _STATES
    ] + [
        pltpu.VMEM((n_rows, N_IN), jnp.float32),
        pltpu.VMEM((n_rows, GROUP), jnp.float32),
        pltpu.VMEM((n_rows, D_MODEL), jnp.bfloat16),
    ]


def _weight_args(w, l):
    specs = [pl.BlockSpec(memory_space=pl.ANY) if n in ("w_in", "w_out") else _layer_spec(w[n], l)
             for n in _WEIGHTS]
    return specs, [w[n] for n in _WEIGHTS]


def _stacked_specs(n_layers, tails):
    def spec(block_tail, tail_index):
        return pl.BlockSpec((n_layers,) + block_tail, lambda *g: (0,) + tail_index(*g))

    return [spec(*tail) for tail in tails]


def _prompt_layer(x, w, l, prev):
    batch, seq, _ = x.shape
    tile = PROMPT_TILE
    assert seq % tile == 0 and tile % CHUNK == 0 and tile >= PAST_A
    tok = pl.BlockSpec((1, tile, D_MODEL), lambda b, j: (b, j, 0))
    tails = [((1, n, GROUP), lambda b, j: (b, 0, 0)) for n, _, _ in _STATES]
    w_specs, w_args = _weight_args(w, l)
    return pl.pallas_call(
        functools.partial(_layer_kernel, sample=False, n_runs=1, seq_len=tile, n_prev=l),
        grid=(batch, seq // tile),
        in_specs=[tok] + w_specs + (_stacked_specs(l, tails) if prev else []),
        out_specs=[tok] + _stacked_specs(l + 1, tails),
        out_shape=[jax.ShapeDtypeStruct(x.shape, x.dtype)]
        + [jax.ShapeDtypeStruct((l + 1, batch, n, GROUP), x.dtype) for n, _, _ in _STATES],
        scratch_shapes=_scratch(1, tile),
        compiler_params=pltpu.CompilerParams(
            dimension_semantics=("arbitrary", "arbitrary"), vmem_limit_bytes=VMEM_LIMIT),
        name="prompt_layer",
    )(x, *w_args, *prev)


def _sample_layers(x2d, states, w, n_steps):
    batch = x2d.shape[0] // n_steps
    depth = states[0].shape[0]
    seqs = SAMPLE_SEQS
    assert batch % seqs == 0 and ROWS % n_steps == 0 and (seqs * n_steps) % CHUNK == 0 and CHUNK % n_steps == 0
    rows = seqs * n_steps
    state_specs = [pl.BlockSpec((1, n, seqs, GROUP), lambda l, i: (l, 0, i, 0)) if step_major
                   else pl.BlockSpec((1, seqs, n, GROUP), lambda l, i: (l, i, 0, 0))
                   for n, _, step_major in _STATES]
    tok = pl.BlockSpec((rows, D_MODEL), lambda l, i: (jnp.where(l == 0, i, batch // seqs - 1), 0))
    tok_out = pl.BlockSpec((rows, D_MODEL), lambda l, i: (jnp.where(l == depth - 1, i, 0), 0))
    w_specs, w_args = _weight_args(w, None)
    return pl.pallas_call(
        functools.partial(_layer_kernel, sample=True, n_runs=seqs, seq_len=n_steps, n_prev=0),
        grid=(depth, batch // seqs),
        in_specs=[tok] + state_specs + w_specs,
        out_specs=[tok_out] + state_specs + [pl.BlockSpec((1, rows, GROUP), lambda l, i: (l, i, 0))],
        out_shape=[jax.ShapeDtypeStruct(x2d.shape, x2d.dtype)]
        + [jax.ShapeDtypeStruct(s.shape, x2d.dtype) for s in states]
        + [jax.ShapeDtypeStruct((depth, batch * n_steps, GROUP), x2d.dtype)],
        scratch_shapes=_scratch(seqs, n_steps) + [pltpu.VMEM((batch // seqs, rows, D_MODEL), jnp.float32)],
        compiler_params=pltpu.CompilerParams(
            dimension_semantics=("arbitrary", "arbitrary"), vmem_limit_bytes=VMEM_LIMIT),
        name="sample_layers",
    )(x2d, *states, *w_args)


def _prepare_weights(pre_norm_g, w_in, conv_a_w, conv_a_b, ln_a_g, ln_a_b, conv_b_w, ln_c_g, ln_c_b,
                     spatial_w, spatial_b, pool_w, pool_scale, w_out, post_norm_g):
    row = lambda v: v[:, None, :]
    depth = w_in.shape[0]
    eye = jnp.eye(N_HEADS, dtype=pool_w.dtype)
    pool_bd = jnp.einsum("lgcd,gh->lgchd", pool_w, eye).reshape(depth, GROUP, GROUP)
    return {
        "pre_g": row(pre_norm_g), "w_in": w_in,
        "ca_w": conv_a_w, "ca_b": row(conv_a_b), "lna_g": row(ln_a_g), "lna_b": row(ln_a_b),
        "cb_w": conv_b_w, "lnc_g": row(ln_c_g), "lnc_b": row(ln_c_b),
        "sp_w": spatial_w, "sp_b": jnp.repeat(spatial_b.transpose(0, 2, 1), HEAD_DIM, axis=2),
        "pl_w": pool_bd.astype(jnp.bfloat16), "pl_s": row(pool_scale),
        "w_out": w_out, "post_g": row(post_norm_g),
    }


def kernel(x_prompt, x_sample, state_conv_a, state_conv_b, state_pool, pre_norm_g, w_in, conv_a_w, conv_a_b, ln_a_g, ln_a_b, conv_b_w, ln_c_g, ln_c_b, spatial_w, spatial_b, pool_w, pool_scale, w_out, post_norm_g):
    depth = w_in.shape[0]
    dec_batch, n_steps, _ = x_sample.shape
    w = _prepare_weights(pre_norm_g, w_in, conv_a_w, conv_a_b, ln_a_g, ln_a_b, conv_b_w, ln_c_g,
                         ln_c_b, spatial_w, spatial_b, pool_w, pool_scale, w_out, post_norm_g)
    step_major = lambda v, flag: jnp.swapaxes(v, 1, 2) if flag else v
    states = [step_major(s, sm) for s, (_, _, sm) in zip((state_conv_a, state_conv_b, state_pool), _STATES)]
    yp, ys = x_prompt, x_sample.reshape(dec_batch * n_steps, D_MODEL)
    prompt_out = []
    for l in range(depth):
        yp, *prompt_out = _prompt_layer(yp, w, l, prompt_out)
    ys, *sample_out = _sample_layers(ys, states, w, n_steps)
    a_s, b_s, d_s = (step_major(s, sm) for s, (_, _, sm) in zip(sample_out[:3], _STATES))
    v_s = sample_out[3].reshape(depth, dec_batch, n_steps, GROUP)
    a_p, b_p, d_p = prompt_out
    return (yp, ys.reshape(x_sample.shape), a_p, a_s, b_p, b_s, d_p, d_s, v_s)
```

```python
import functools
import types

import jax
import jax.numpy as jnp
from jax import lax
from jax.experimental import pallas as pl
from jax.experimental.pallas import tpu as pltpu

D_MODEL = 1024
GROUP = 256
HEAD_DIM = 64
N_HEADS = 4
CONV_A_WIDTH = 31
CONV_B_WIDTH = 3
CHUNK = 128
POOL_BUF = 15
EPS = 1e-6
LANES = 128
BF16_ROWS = 16
HALVES = GROUP // LANES

(A_VAL, A_GATE, Z_A, B_B, B_C, B_X, Z_B, C_U, C_V, Z_C, D_X, Z_D) = range(12)
N_IN = 12 * GROUP
SPATIAL_OUT = C_V
POOL_OUT = D_X

PAST_A = 32
PAST_B = 8
PAST_D = 16

PROMPT_TILE = 1024
SAMPLE_SEQS = 64
ROWS = 32

VMEM_LIMIT = 58 * 1024 * 1024


def _sigmoid(v):
    return 1.0 / (1.0 + jnp.exp(-v))


def _silu(v):
    return v * _sigmoid(v)


def _rowsum(vs):
    tot = vs[0]
    for v in vs[1:]:
        tot = tot + v
    return jnp.sum(tot, axis=-1, keepdims=True)


def _half(h):
    return slice(h * LANES, (h + 1) * LANES)


def _slot(slot, h):
    return slice(slot * GROUP + h * LANES, slot * GROUP + (h + 1) * LANES)


def _layernorm2(v0, v1, g_ref, b_ref):
    mu = _rowsum([v0, v1]) * (1.0 / GROUP)
    d0, d1 = v0 - mu, v1 - mu
    var = _rowsum([d0 * d0, d1 * d1]) * (1.0 / GROUP)
    inv = lax.rsqrt(var + EPS)
    return (d0 * inv * g_ref[:, _half(0)] + b_ref[:, _half(0)],
            d1 * inv * g_ref[:, _half(1)] + b_ref[:, _half(1)])


def _copy_rows(dst, src, n_rows, chunk):
    def body(i, carry):
        r = pl.multiple_of(i * chunk, chunk)
        dst[pl.ds(r, chunk), :] = src[pl.ds(r, chunk), :].astype(dst.dtype)
        return carry
    lax.fori_loop(0, n_rows // chunk, body, 0)


class _Row:
    def __init__(self, ref, i, width):
        self.ref, self.i, self.width = ref, i, width

    def __getitem__(self, idx):
        lanes = slice(0, self.width) if idx is Ellipsis else idx[1]
        return self.ref[self.i:self.i + 1, lanes]


class _History:
    def __init__(self, ref, past, seq_len):
        self.ref, self.past, self.seq_len = ref, past, seq_len
        self.piece = min(seq_len, ROWS)

    def _offsets(self, r0):
        for i in range(ROWS // self.piece):
            run, step = divmod(r0 + i * self.piece, self.seq_len)
            yield i, run * (self.past + self.seq_len) + self.past + step

    def window(self, h, r0, back, extra=0):
        assert extra == 0 or self.piece == ROWS
        parts = [self.ref[h, off - back - extra:off - back + self.piece, :] for _, off in self._offsets(r0)]
        return parts[0] if len(parts) == 1 else jnp.concatenate(parts, axis=0)

    def store(self, h, r0, value):
        for i, off in self._offsets(r0):
            self.ref[h, off:off + self.piece, :] = value[i * self.piece:(i + 1) * self.piece]

    def tail(self, h, run, n):
        end = (run + 1) * (self.past + self.seq_len)
        return self.ref[h, end - n:end, :]

    def set_history(self, h, run, value):
        start = run * (self.past + self.seq_len) + self.past
        self.ref[h, start - value.shape[0]:start, :] = value

    def _across_runs(self, row, n_runs):
        return pl.ds(row, n_runs, stride=self.past + self.seq_len)

    def tail_step(self, h, n_runs, j, n):
        return self.ref.at[h][self._across_runs(self.past + self.seq_len - n + j, n_runs), :]

    def set_history_step(self, h, j, n, value):
        self.ref.at[h][self._across_runs(self.past - n + j, value.shape[0]), :] = value


def _phase_in(t, p):
    for r0 in range(0, t.n_rows, ROWS):
        xb = t.x[r0:r0 + ROWS, :]
        ms = jnp.sum(xb * xb, axis=-1, keepdims=True) * (1.0 / D_MODEL)
        hb = xb * lax.rsqrt(ms + EPS) * p.pre_g[...]
        t.mix[r0:r0 + ROWS, :] = hb.astype(jnp.bfloat16)
    t.proj[...] = jnp.dot(t.mix[...], p.w_in[...], preferred_element_type=jnp.float32)


def _phase_mix(t, p):
    lo = lax.broadcasted_iota(jnp.int32, (ROWS, LANES), 1) < HEAD_DIM
    inv_w = [jnp.where(lo, 1.0 / 2, 1.0 / 4), jnp.where(lo, 1.0 / 8, 1.0 / 16)]
    win = [jnp.where(lo, 2, 4), jnp.where(lo, 8, 16)]
    hist_a, hist_b, hist_d = t.hist_a, t.hist_b, t.hist_d

    def pj(r0, slot, h):
        return t.proj[r0:r0 + ROWS, _slot(slot, h)]

    for r0 in range(0, t.n_rows, ROWS):
        for h in range(HALVES):
            hist_a.store(h, r0, pj(r0, A_VAL, h) * _sigmoid(pj(r0, A_GATE, h)))
            hist_b.store(h, r0, pj(r0, B_C, h) * pj(r0, B_X, h))
            hist_d.store(h, r0, pj(r0, D_X, h))
        v0, v1 = _layernorm2(pj(r0, C_V, 0), pj(r0, C_V, 1), p.lnc_g, p.lnc_b)
        t.vn[r0:r0 + ROWS, _half(0)] = v0
        t.vn[r0:r0 + ROWS, _half(1)] = v1

        a2 = hist_d.window(0, r0, 0) + hist_d.window(0, r0, 1)
        a4 = a2 + (hist_d.window(0, r0, 2) + hist_d.window(0, r0, 3))
        sums = [jnp.where(lo, a2, a4)]
        if hist_d.piece == ROWS:
            s8 = hist_d.window(1, r0, 0, extra=8)
            for j in range(1, 8):
                s8 = s8 + hist_d.window(1, r0, j, extra=8)
            s8_cur, s8_prev = s8[8:], s8[:ROWS]
        else:
            s8_cur, s8_prev = hist_d.window(1, r0, 0), hist_d.window(1, r0, 8)
            for j in range(1, 8):
                s8_cur = s8_cur + hist_d.window(1, r0, j)
                s8_prev = s8_prev + hist_d.window(1, r0, 8 + j)
        sums.append(jnp.where(lo, s8_cur, s8_cur + s8_prev))
        cnt = t.first_block_cnt() if r0 == 0 else None
        for h in range(HALVES):
            if cnt is None:
                mean = sums[h] * inv_w[h]
            else:
                mean = sums[h] / jnp.minimum(cnt, win[h]).astype(jnp.float32)
            t.mix[r0:r0 + ROWS, _slot(3, h)] = (mean - hist_d.window(h, r0, 0)).astype(jnp.bfloat16)

    lo_chunk = lax.broadcasted_iota(jnp.int32, (CHUNK, LANES), 1) < HEAD_DIM
    for c0 in range(0, t.n_rows, CHUNK):
        for h in range(HALVES):
            vb = t.vn[c0:c0 + CHUNK, _half(h)].astype(jnp.bfloat16)
            s_even = jnp.dot(p.sp_w[2 * h], vb, preferred_element_type=jnp.float32)
            s_odd = jnp.dot(p.sp_w[2 * h + 1], vb, preferred_element_type=jnp.float32)
            t.proj[c0:c0 + CHUNK, _slot(SPATIAL_OUT, h)] = (
                jnp.where(lo_chunk, s_even, s_odd) + t.chunk_bias(h))
    t.proj[:, POOL_OUT * GROUP:(POOL_OUT + 1) * GROUP] = jnp.dot(
        t.mix[:, 3 * GROUP:4 * GROUP], p.pl_w[...], preferred_element_type=jnp.float32)

    for r0 in range(0, t.n_rows, ROWS):
        _mix_block(t, p, r0)
    t.proj[:, 0:D_MODEL] = jnp.dot(t.mix[...], p.w_out[...], preferred_element_type=jnp.float32)


def _mix_block(t, p, r0):
    hist_a, hist_b = t.hist_a, t.hist_b

    def pj(r0, slot, h):
        return t.proj[r0:r0 + ROWS, _slot(slot, h)]

    ya = []
    for h in range(HALVES):
        acc = jnp.broadcast_to(p.ca_b[:, _half(h)], (ROWS, LANES))
        for k in range(CONV_A_WIDTH):
            acc = acc + p.ca_w[k:k + 1, _half(h)] * hist_a.window(h, r0, CONV_A_WIDTH - 1 - k)
        ya.append(acc)
    na = _layernorm2(ya[0], ya[1], p.lna_g, p.lna_b)
    for h in range(HALVES):
        t.mix[r0:r0 + ROWS, _slot(0, h)] = (_silu(na[h]) * _silu(pj(r0, Z_A, h))).astype(jnp.bfloat16)

    for h in range(HALVES):
        acc = None
        for k in range(CONV_B_WIDTH):
            term = p.cb_w[k:k + 1, _half(h)] * hist_b.window(h, r0, CONV_B_WIDTH - 1 - k)
            acc = term if acc is None else acc + term
        yb = pj(r0, B_B, h) * acc * _silu(pj(r0, Z_B, h))
        t.mix[r0:r0 + ROWS, _slot(1, h)] = yb.astype(jnp.bfloat16)

    for h in range(HALVES):
        yc = pj(r0, C_U, h) * pj(r0, SPATIAL_OUT, h) * _silu(pj(r0, Z_C, h))
        t.mix[r0:r0 + ROWS, _slot(2, h)] = yc.astype(jnp.bfloat16)
        yd = pj(r0, POOL_OUT, h) * p.pl_s[:, _half(h)] * _silu(pj(r0, Z_D, h))
        t.mix[r0:r0 + ROWS, _slot(3, h)] = yd.astype(jnp.bfloat16)


def _phase_out(t, p):
    for r0 in range(0, t.n_rows, ROWS):
        ob = t.proj[r0:r0 + ROWS, 0:D_MODEL]
        ms = jnp.sum(ob * ob, axis=-1, keepdims=True) * (1.0 / D_MODEL)
        yb = t.x[r0:r0 + ROWS, :] + ob * lax.rsqrt(ms + EPS) * p.post_g[...]
        for y in t.ys:
            y[r0:r0 + ROWS, :] = yb


_WEIGHTS = ("pre_g", "w_in", "ca_w", "ca_b", "lna_g", "lna_b", "cb_w", "lnc_g", "lnc_b", "sp_w", "sp_b",
            "pl_w", "pl_s", "w_out", "post_g")
_STATES = ((CONV_A_WIDTH - 1, PAST_A, True), (CONV_B_WIDTH - 1, PAST_B, False), (POOL_BUF, PAST_D, True))


def _layer_kernel(*refs, sample, n_runs, seq_len, n_prev):
    n_state, n_w = len(_STATES), len(_WEIGHTS)
    n_stacked = n_state + (1 if sample else 0)
    n_carried = n_stacked if n_prev else 0
    n_in = 1 + (n_state if sample else 0) + n_w + n_carried
    n_out = 1 + n_stacked
    x_ref, state_in = refs[0], refs[1:n_in - n_w - n_carried]
    w = dict(zip(_WEIGHTS, refs[n_in - n_w - n_carried:n_in - n_carried]))
    carried = refs[n_in - n_carried:n_in]
    y_ref, stacked_out = refs[n_in], refs[n_in + 1:n_in + n_out]
    state_out = stacked_out[:n_state]
    scratch = refs[n_in + n_out:]
    w_in, w_out, sp_w, vec_l, ca_w_l, cb_w_l, ext_a, ext_b, ext_d, proj, vn_ref, mix = scratch[:12]
    n_rows = n_runs * seq_len
    hists = [_History(ext, past, seq_len) for ext, (_, past, _) in zip((ext_a, ext_b, ext_d), _STATES)]

    if sample:
        first = pl.program_id(1) == 0
        tokens = scratch[12].at[pl.program_id(1)]

        @pl.when(pl.program_id(0) == 0)
        def _():
            tokens[...] = x_ref[...]
    else:
        j = pl.program_id(1)
        first = (pl.program_id(0) == 0) & (j == 0)

    @pl.when(first)
    def _():
        _copy_rows(w_in, w["w_in"].at[0], D_MODEL, BF16_ROWS)
        _copy_rows(w_out, w["w_out"].at[0], D_MODEL, BF16_ROWS)
        r_i = lax.broadcasted_iota(jnp.int32, (CHUNK, CHUNK), 0)
        c_i = lax.broadcasted_iota(jnp.int32, (CHUNK, CHUNK), 1)
        keep = c_i <= r_i
        if seq_len < CHUNK:
            keep = keep & (r_i // seq_len == c_i // seq_len)
            in_corner = lax.broadcasted_iota(jnp.int32, (seq_len, CHUNK), 1) < seq_len
        for hd in range(N_HEADS):
            mat = w["sp_w"][0, hd]
            if seq_len < CHUNK:
                corner = jnp.where(in_corner, mat[0:seq_len, :], 0.0)
                lanes = corner
                for q in range(1, CHUNK // seq_len):
                    lanes = lanes + pltpu.roll(corner, q * seq_len, axis=1)
                mat = jnp.concatenate([lanes] * (CHUNK // seq_len), axis=0)
            sp_w[hd] = jnp.where(keep, mat, 0.0).astype(jnp.bfloat16)

    if sample:
        for hist, ref, (n, _, step_major) in zip(hists, state_in, _STATES):
            for h in range(HALVES):
                if step_major:
                    for s in range(n):
                        hist.set_history_step(h, s, n, ref[0, s, :, _half(h)])
                else:
                    for run in range(n_runs):
                        hist.set_history(h, run, ref[0, run, :, _half(h)])
        first_block_cnt = lambda: None
    else:
        @pl.when(j == 0)
        def _():
            for hist in hists:
                hist.ref[:, 0:hist.past, :] = jnp.zeros((HALVES, hist.past, LANES), jnp.float32)

        @pl.when(j > 0)
        def _():
            for hist in hists:
                hist.ref[:, 0:hist.past, :] = hist.ref[:, seq_len:seq_len + hist.past, :]

        def first_block_cnt():
            return j * seq_len + lax.broadcasted_iota(jnp.int32, (ROWS, LANES), 0) + 1

    layer = pl.program_id(0) if sample else n_prev

    def of_layer(get):
        if isinstance(layer, int):
            return get(layer)
        val = get(0)
        for k in range(1, w["pre_g"].shape[0]):
            val = jnp.where(layer == k, get(k), val)
        return val

    rows = {}
    for i, name in enumerate(_ROW_VECTORS):
        width = w[name].shape[1]
        vec_l[i:i + 1, 0:width] = of_layer(lambda k, name=name: w[name][k:k + 1, :])
        rows[name] = _Row(vec_l, i, width)
    ca_w_l[...] = of_layer(lambda k: w["ca_w"][:, k, :])
    cb_w_l[...] = of_layer(lambda k: w["cb_w"][:, k, :])
    block = lambda name: w[name].at[0]
    p = types.SimpleNamespace(
        w_in=w_in, w_out=w_out, sp_w=sp_w, ca_w=ca_w_l, cb_w=cb_w_l, sp_b=block("sp_b"), pl_w=block("pl_w"),
        **rows)

    def chunk_bias(h):
        steps = min(seq_len, CHUNK)
        return jnp.concatenate([p.sp_b[0:steps, _half(h)]] * (CHUNK // steps), axis=0)

    t = types.SimpleNamespace(
        x=tokens if sample else x_ref.at[0], ys=(tokens, y_ref) if sample else (y_ref.at[0],),
        hist_a=hists[0], hist_b=hists[1], hist_d=hists[2], proj=proj, vn=vn_ref, mix=mix,
        n_rows=n_rows, first_block_cnt=first_block_cnt, chunk_bias=chunk_bias)
    _phase_in(t, p)
    _phase_mix(t, p)
    _phase_out(t, p)

    def write_stacked():
        for ref, src in zip(stacked_out, carried):
            for k in range(n_prev):
                ref[k] = src[k]
        for hist, ref, (n, _, step_major) in zip(hists, state_out, _STATES):
            for h in range(HALVES):
                if sample and step_major:
                    for s in range(n):
                        ref[n_prev, s, :, _half(h)] = hist.tail_step(h, n_runs, s, n)
                elif step_major:
                    row0 = hist.past + seq_len - n
                    for b in range(ref.shape[2]):
                        @pl.when(pl.program_id(0) == b)
                        def _(b=b, ref=ref, hist=hist, h=h, row0=row0, n=n):
                            for s in range(n):
                                ref[n_prev, s, b:b + 1, _half(h)] = hist.ref[h, row0 + s:row0 + s + 1, :]
                else:
                    for run in range(n_runs):
                        ref[n_prev, run, :, _half(h)] = hist.tail(h, run, n)
        if sample:
            stacked_out[n_state][n_prev] = vn_ref[...]

    if sample:
        write_stacked()
    else:
        pl.when(j == pl.num_programs(1) - 1)(write_stacked)


def _layer_spec(arr, l, single_buffer=False):
    shape = (1,) + arr.shape[1:]
    index_map = lambda *g: (g[0] if l is None else l,) + (0,) * (len(shape) - 1)
    if single_buffer:
        return pl.BlockSpec(shape, index_map, pipeline_mode=pl.Buffered(1))
    return pl.BlockSpec(shape, index_map)


def _scratch(n_runs, seq_len):
    n_rows = n_runs * seq_len
    return [
        pltpu.VMEM((D_MODEL, N_IN), jnp.bfloat16),
        pltpu.VMEM((D_MODEL, D_MODEL), jnp.bfloat16),
        pltpu.VMEM((N_HEADS, CHUNK, CHUNK), jnp.bfloat16),
        pltpu.VMEM((len(_ROW_VECTORS), D_MODEL), jnp.float32),
        pltpu.VMEM((CONV_A_WIDTH, GROUP), jnp.float32),
        pltpu.VMEM((CONV_B_WIDTH, GROUP), jnp.float32),
    ] + [
        pltpu.VMEM((HALVES, n_runs * (past + seq_len), LANES), jnp.float32) for _, past, _ in _STATES
    ] + [
        pltpu.VMEM((n_rows, N_IN), jnp.float32),
        pltpu.VMEM((n_rows, GROUP), jnp.float32),
        pltpu.VMEM((n_rows, D_MODEL), jnp.bfloat16),
    ]


_LAYER_BLOCKS = ("w_in", "w_out", "sp_w", "sp_b", "pl_w")
_ROW_VECTORS = ("pre_g", "post_g", "ca_b", "lna_g", "lna_b", "lnc_g", "lnc_b", "pl_s")


def _weight_args(w, l):
    def spec(n):
        if n in _LAYER_BLOCKS:
            return _layer_spec(w[n], l, single_buffer=n in ("w_in", "w_out"))
        return pl.BlockSpec(w[n].shape, lambda *g: (0,) * w[n].ndim)

    return [spec(n) for n in _WEIGHTS], [w[n] for n in _WEIGHTS]


def _stacked_specs(n_layers, tails):
    def spec(block_tail, tail_index):
        return pl.BlockSpec((n_layers,) + block_tail, lambda *g: (0,) + tail_index(*g))

    return [spec(*tail) for tail in tails]


def _prompt_layer(x, w, l, prev):
    batch, seq, _ = x.shape
    tile = PROMPT_TILE
    assert seq % tile == 0 and tile % CHUNK == 0 and tile >= PAST_A
    tok = pl.BlockSpec((1, tile, D_MODEL), lambda b, j: (b, j, 0))
    tails = [((n, batch, GROUP), lambda b, j: (0, 0, 0)) if step_major else ((1, n, GROUP), lambda b, j: (b, 0, 0))
             for n, _, step_major in _STATES]
    w_specs, w_args = _weight_args(w, l)
    return pl.pallas_call(
        functools.partial(_layer_kernel, sample=False, n_runs=1, seq_len=tile, n_prev=l),
        grid=(batch, seq // tile),
        in_specs=[tok] + w_specs + (_stacked_specs(l, tails) if prev else []),
        out_specs=[tok] + _stacked_specs(l + 1, tails),
        out_shape=[jax.ShapeDtypeStruct(x.shape, x.dtype)]
        + [jax.ShapeDtypeStruct((l + 1, n, batch, GROUP) if step_major else (l + 1, batch, n, GROUP), x.dtype)
           for n, _, step_major in _STATES],
        scratch_shapes=_scratch(1, tile),
        compiler_params=pltpu.CompilerParams(
            dimension_semantics=("arbitrary", "arbitrary"), vmem_limit_bytes=VMEM_LIMIT),
        name="prompt_layer",
    )(x, *w_args, *prev)


def _sample_layers(x2d, states, w, n_steps):
    batch = x2d.shape[0] // n_steps
    depth = states[0].shape[0]
    seqs = SAMPLE_SEQS
    assert batch % seqs == 0 and ROWS % n_steps == 0 and (seqs * n_steps) % CHUNK == 0 and CHUNK % n_steps == 0
    rows = seqs * n_steps
    state_specs = [pl.BlockSpec((1, n, seqs, GROUP), lambda l, i: (l, 0, i, 0)) if step_major
                   else pl.BlockSpec((1, seqs, n, GROUP), lambda l, i: (l, i, 0, 0))
                   for n, _, step_major in _STATES]
    tok = pl.BlockSpec((rows, D_MODEL), lambda l, i: (jnp.where(l == 0, i, batch // seqs - 1), 0))
    tok_out = pl.BlockSpec((rows, D_MODEL), lambda l, i: (jnp.where(l == depth - 1, i, 0), 0))
    w_specs, w_args = _weight_args(w, None)
    return pl.pallas_call(
        functools.partial(_layer_kernel, sample=True, n_runs=seqs, seq_len=n_steps, n_prev=0),
        grid=(depth, batch // seqs),
        in_specs=[tok] + state_specs + w_specs,
        out_specs=[tok_out] + state_specs + [pl.BlockSpec((1, rows, GROUP), lambda l, i: (l, i, 0))],
        out_shape=[jax.ShapeDtypeStruct(x2d.shape, x2d.dtype)]
        + [jax.ShapeDtypeStruct(s.shape, x2d.dtype) for s in states]
        + [jax.ShapeDtypeStruct((depth, batch * n_steps, GROUP), x2d.dtype)],
        scratch_shapes=_scratch(seqs, n_steps) + [pltpu.VMEM((batch // seqs, rows, D_MODEL), jnp.float32)],
        compiler_params=pltpu.CompilerParams(
            dimension_semantics=("arbitrary", "arbitrary"), vmem_limit_bytes=VMEM_LIMIT),
        name="sample_layers",
    )(x2d, *states, *w_args)


def _prepare_weights(pre_norm_g, w_in, conv_a_w, conv_a_b, ln_a_g, ln_a_b, conv_b_w, ln_c_g, ln_c_b,
                     spatial_w, spatial_b, pool_w, pool_scale, w_out, post_norm_g):
    depth = w_in.shape[0]
    eye = jnp.eye(N_HEADS, dtype=pool_w.dtype)
    pool_bd = jnp.einsum("lgcd,gh->lgchd", pool_w, eye).reshape(depth, GROUP, GROUP)
    return {
        "pre_g": pre_norm_g, "w_in": w_in.astype(jnp.bfloat16),
        "ca_w": jnp.swapaxes(conv_a_w, 0, 1), "ca_b": conv_a_b, "lna_g": ln_a_g, "lna_b": ln_a_b,
        "cb_w": jnp.swapaxes(conv_b_w, 0, 1), "lnc_g": ln_c_g, "lnc_b": ln_c_b,
        "sp_w": spatial_w, "sp_b": jnp.repeat(spatial_b.transpose(0, 2, 1), HEAD_DIM, axis=2),
        "pl_w": pool_bd.astype(jnp.bfloat16), "pl_s": pool_scale,
        "w_out": w_out.astype(jnp.bfloat16), "post_g": post_norm_g,
    }


def kernel(x_prompt, x_sample, state_conv_a, state_conv_b, state_pool, pre_norm_g, w_in, conv_a_w, conv_a_b, ln_a_g, ln_a_b, conv_b_w, ln_c_g, ln_c_b, spatial_w, spatial_b, pool_w, pool_scale, w_out, post_norm_g):
    depth = w_in.shape[0]
    dec_batch, n_steps, _ = x_sample.shape
    w = _prepare_weights(pre_norm_g, w_in, conv_a_w, conv_a_b, ln_a_g, ln_a_b, conv_b_w, ln_c_g,
                         ln_c_b, spatial_w, spatial_b, pool_w, pool_scale, w_out, post_norm_g)
    step_major = lambda v, flag: jnp.swapaxes(v, 1, 2) if flag else v
    states = [step_major(s, sm) for s, (_, _, sm) in zip((state_conv_a, state_conv_b, state_pool), _STATES)]
    yp, ys = x_prompt, x_sample.reshape(dec_batch * n_steps, D_MODEL)
    prompt_out = []
    for l in range(depth):
        yp, *prompt_out = _prompt_layer(yp, w, l, prompt_out)
    ys, *sample_out = _sample_layers(ys, states, w, n_steps)
    a_s, b_s, d_s = (step_major(s, sm) for s, (_, _, sm) in zip(sample_out[:3], _STATES))
    v_s = sample_out[3].reshape(depth, dec_batch, n_steps, GROUP)
    a_p, b_p, d_p = (step_major(s, sm) for s, (_, _, sm) in zip(prompt_out, _STATES))
    return (yp, ys.reshape(x_sample.shape), a_p, a_s, b_p, b_s, d_p, d_s, v_s)
```

```python
import functools
import types

import jax
import jax.numpy as jnp
from jax import lax
from jax.experimental import pallas as pl
from jax.experimental.pallas import tpu as pltpu

D_MODEL = 1024
GROUP = 256
HEAD_DIM = 64
N_HEADS = 4
CONV_A_WIDTH = 31
CONV_B_WIDTH = 3
CHUNK = 128
POOL_BUF = 15
EPS = 1e-6
LANES = 128
BF16_ROWS = 16
HALVES = GROUP // LANES

(A_VAL, A_GATE, Z_A, B_B, B_C, B_X, Z_B, C_U, C_V, Z_C, D_X, Z_D) = range(12)
N_IN = 12 * GROUP
SPATIAL_OUT = C_V
POOL_OUT = D_X

PAST_A = 32
PAST_B = 8
PAST_D = 16

PROMPT_TILE = 1024
SAMPLE_SEQS = 64
ROWS = 32

VMEM_LIMIT = 61 * 1024 * 1024


def _sigmoid(v):
    return 1.0 / (1.0 + jnp.exp(-v))


def _silu(v):
    return v * _sigmoid(v)


def _rowsum(vs):
    tot = vs[0]
    for v in vs[1:]:
        tot = tot + v
    return jnp.sum(tot, axis=-1, keepdims=True)


def _half(h):
    return slice(h * LANES, (h + 1) * LANES)


def _slot(slot, h):
    return slice(slot * GROUP + h * LANES, slot * GROUP + (h + 1) * LANES)


def _layernorm2(v0, v1, g_ref, b_ref):
    mu = _rowsum([v0, v1]) * (1.0 / GROUP)
    d0, d1 = v0 - mu, v1 - mu
    var = _rowsum([d0 * d0, d1 * d1]) * (1.0 / GROUP)
    inv = lax.rsqrt(var + EPS)
    return (d0 * inv * g_ref[:, _half(0)] + b_ref[:, _half(0)],
            d1 * inv * g_ref[:, _half(1)] + b_ref[:, _half(1)])


def _copy_rows(dst, src, n_rows, chunk):
    def body(i, carry):
        r = pl.multiple_of(i * chunk, chunk)
        dst[pl.ds(r, chunk), :] = src[pl.ds(r, chunk), :].astype(dst.dtype)
        return carry
    lax.fori_loop(0, n_rows // chunk, body, 0)


class _Row:
    def __init__(self, ref, i, width):
        self.ref, self.i, self.width = ref, i, width

    def __getitem__(self, idx):
        lanes = slice(0, self.width) if idx is Ellipsis else idx[1]
        return self.ref[self.i:self.i + 1, lanes]


class _History:
    def __init__(self, ref, past, seq_len):
        self.ref, self.past, self.seq_len = ref, past, seq_len
        self.piece = min(seq_len, ROWS)

    def _offsets(self, r0):
        for i in range(ROWS // self.piece):
            run, step = divmod(r0 + i * self.piece, self.seq_len)
            yield i, run * (self.past + self.seq_len) + self.past + step

    def window(self, h, r0, back, extra=0):
        assert extra == 0 or self.piece == ROWS
        parts = [self.ref[h, off - back - extra:off - back + self.piece, :] for _, off in self._offsets(r0)]
        return parts[0] if len(parts) == 1 else jnp.concatenate(parts, axis=0)

    def store(self, h, r0, value):
        for i, off in self._offsets(r0):
            self.ref[h, off:off + self.piece, :] = value[i * self.piece:(i + 1) * self.piece]

    def tail(self, h, run, n):
        end = (run + 1) * (self.past + self.seq_len)
        return self.ref[h, end - n:end, :]

    def set_history(self, h, run, value):
        start = run * (self.past + self.seq_len) + self.past
        self.ref[h, start - value.shape[0]:start, :] = value

    def _across_runs(self, row, n_runs):
        return pl.ds(row, n_runs, stride=self.past + self.seq_len)

    def tail_step(self, h, n_runs, j, n):
        return self.ref.at[h][self._across_runs(self.past + self.seq_len - n + j, n_runs), :]

    def set_history_step(self, h, j, n, value):
        self.ref.at[h][self._across_runs(self.past - n + j, value.shape[0]), :] = value


def _prenorm(x, h, p):
    for r0 in range(0, x.shape[0], ROWS):
        xb = x[r0:r0 + ROWS, :]
        ms = jnp.sum(xb * xb, axis=-1, keepdims=True) * (1.0 / D_MODEL)
        hb = xb * lax.rsqrt(ms + EPS) * p.pre_g[...]
        h[r0:r0 + ROWS, :] = hb.astype(jnp.bfloat16)


def _postnorm(x, ys, out, p):
    for r0 in range(0, x.shape[0], ROWS):
        ob = out[r0:r0 + ROWS, :]
        ms = jnp.sum(ob * ob, axis=-1, keepdims=True) * (1.0 / D_MODEL)
        yb = x[r0:r0 + ROWS, :] + ob * lax.rsqrt(ms + EPS) * p.post_g[...]
        for y in ys:
            y[r0:r0 + ROWS, :] = yb


def _phase_in(t, p):
    _prenorm(t.x, t.mix, p)
    t.proj[...] = jnp.dot(t.mix[...], p.w_in[...], preferred_element_type=jnp.float32)


def _phase_mix(t, p):
    lo = lax.broadcasted_iota(jnp.int32, (ROWS, LANES), 1) < HEAD_DIM
    inv_w = [jnp.where(lo, 1.0 / 2, 1.0 / 4), jnp.where(lo, 1.0 / 8, 1.0 / 16)]
    win = [jnp.where(lo, 2, 4), jnp.where(lo, 8, 16)]
    hist_a, hist_b, hist_d = t.hist_a, t.hist_b, t.hist_d

    def pj(r0, slot, h):
        return t.proj[r0:r0 + ROWS, _slot(slot, h)]

    for r0 in range(0, t.n_rows, ROWS):
        for h in range(HALVES):
            hist_a.store(h, r0, pj(r0, A_VAL, h) * _sigmoid(pj(r0, A_GATE, h)))
            hist_b.store(h, r0, pj(r0, B_C, h) * pj(r0, B_X, h))
            hist_d.store(h, r0, pj(r0, D_X, h))
        v0, v1 = _layernorm2(pj(r0, C_V, 0), pj(r0, C_V, 1), p.lnc_g, p.lnc_b)
        t.vn[r0:r0 + ROWS, _half(0)] = v0
        t.vn[r0:r0 + ROWS, _half(1)] = v1

        a2 = hist_d.window(0, r0, 0) + hist_d.window(0, r0, 1)
        a4 = a2 + (hist_d.window(0, r0, 2) + hist_d.window(0, r0, 3))
        sums = [jnp.where(lo, a2, a4)]
        if hist_d.piece == ROWS:
            s8 = hist_d.window(1, r0, 0, extra=8)
            for j in range(1, 8):
                s8 = s8 + hist_d.window(1, r0, j, extra=8)
            s8_cur, s8_prev = s8[8:], s8[:ROWS]
        else:
            s8_cur, s8_prev = hist_d.window(1, r0, 0), hist_d.window(1, r0, 8)
            for j in range(1, 8):
                s8_cur = s8_cur + hist_d.window(1, r0, j)
                s8_prev = s8_prev + hist_d.window(1, r0, 8 + j)
        sums.append(jnp.where(lo, s8_cur, s8_cur + s8_prev))
        cnt = t.first_block_cnt() if r0 == 0 else None
        for h in range(HALVES):
            if cnt is None:
                mean = sums[h] * inv_w[h]
            else:
                mean = sums[h] / jnp.minimum(cnt, win[h]).astype(jnp.float32)
            t.mix[r0:r0 + ROWS, _slot(3, h)] = (mean - hist_d.window(h, r0, 0)).astype(jnp.bfloat16)

    lo_chunk = lax.broadcasted_iota(jnp.int32, (CHUNK, LANES), 1) < HEAD_DIM
    for c0 in range(0, t.n_rows, CHUNK):
        for h in range(HALVES):
            vb = t.vn[c0:c0 + CHUNK, _half(h)].astype(jnp.bfloat16)
            s_even = jnp.dot(p.sp_w[2 * h], vb, preferred_element_type=jnp.float32)
            s_odd = jnp.dot(p.sp_w[2 * h + 1], vb, preferred_element_type=jnp.float32)
            t.proj[c0:c0 + CHUNK, _slot(SPATIAL_OUT, h)] = (
                jnp.where(lo_chunk, s_even, s_odd) + t.chunk_bias(h))
    t.proj[:, POOL_OUT * GROUP:(POOL_OUT + 1) * GROUP] = jnp.dot(
        t.mix[:, 3 * GROUP:4 * GROUP], p.pl_w[...], preferred_element_type=jnp.float32)

    for r0 in range(0, t.n_rows, ROWS):
        _mix_block(t, p, r0)
    t.proj[:, t.out_cols] = jnp.dot(t.mix[...], p.w_out[...], preferred_element_type=jnp.float32)


def _mix_block(t, p, r0):
    hist_a, hist_b = t.hist_a, t.hist_b

    def pj(r0, slot, h):
        return t.proj[r0:r0 + ROWS, _slot(slot, h)]

    ya = []
    for h in range(HALVES):
        acc = jnp.broadcast_to(p.ca_b[:, _half(h)], (ROWS, LANES))
        for k in range(CONV_A_WIDTH):
            acc = acc + p.ca_w[k:k + 1, _half(h)] * hist_a.window(h, r0, CONV_A_WIDTH - 1 - k)
        ya.append(acc)
    na = _layernorm2(ya[0], ya[1], p.lna_g, p.lna_b)
    for h in range(HALVES):
        t.mix[r0:r0 + ROWS, _slot(0, h)] = (_silu(na[h]) * _silu(pj(r0, Z_A, h))).astype(jnp.bfloat16)

    for h in range(HALVES):
        acc = None
        for k in range(CONV_B_WIDTH):
            term = p.cb_w[k:k + 1, _half(h)] * hist_b.window(h, r0, CONV_B_WIDTH - 1 - k)
            acc = term if acc is None else acc + term
        yb = pj(r0, B_B, h) * acc * _silu(pj(r0, Z_B, h))
        t.mix[r0:r0 + ROWS, _slot(1, h)] = yb.astype(jnp.bfloat16)

    for h in range(HALVES):
        yc = pj(r0, C_U, h) * pj(r0, SPATIAL_OUT, h) * _silu(pj(r0, Z_C, h))
        t.mix[r0:r0 + ROWS, _slot(2, h)] = yc.astype(jnp.bfloat16)
        yd = pj(r0, POOL_OUT, h) * p.pl_s[:, _half(h)] * _silu(pj(r0, Z_D, h))
        t.mix[r0:r0 + ROWS, _slot(3, h)] = yd.astype(jnp.bfloat16)


def _phase_out(t, p):
    _postnorm(t.x, t.ys, t.proj.at[:, t.out_cols], p)


_WEIGHTS = ("pre_g", "w_in", "ca_w", "ca_b", "lna_g", "lna_b", "cb_w", "lnc_g", "lnc_b", "sp_w", "sp_b",
            "pl_w", "pl_s", "w_out", "post_g")
_STATES = ((CONV_A_WIDTH - 1, PAST_A, True), (CONV_B_WIDTH - 1, PAST_B, False), (POOL_BUF, PAST_D, True))


def _layer_kernel(*refs, sample, n_runs, seq_len, n_prev):
    n_state, n_w = len(_STATES), len(_WEIGHTS)
    n_stacked = n_state + (1 if sample else 0)
    n_carried = n_stacked if n_prev else 0
    n_in = 1 + (n_state if sample else 0) + n_w + n_carried
    n_out = 1 + n_stacked
    x_ref, state_in = refs[0], refs[1:n_in - n_w - n_carried]
    w = dict(zip(_WEIGHTS, refs[n_in - n_w - n_carried:n_in - n_carried]))
    carried = refs[n_in - n_carried:n_in]
    y_ref, stacked_out = refs[n_in], refs[n_in + 1:n_in + n_out]
    state_out = stacked_out[:n_state]
    scratch = refs[n_in + n_out:]
    w_in, w_out, sp_w, vec_l, ca_w_l, cb_w_l, ext_a, ext_b, ext_d, proj, vn_ref, mix = scratch[:12]
    n_rows = n_runs * seq_len
    hists = [_History(ext, past, seq_len) for ext, (_, past, _) in zip((ext_a, ext_b, ext_d), _STATES)]

    if sample:
        first = pl.program_id(1) == 0
        tokens = scratch[12].at[pl.program_id(1)]

        @pl.when(pl.program_id(0) == 0)
        def _():
            tokens[...] = x_ref[...]
    else:
        j = pl.program_id(1)
        first = (pl.program_id(0) == 0) & (j == 0)

    @pl.when(first)
    def _():
        _copy_rows(w_in, w["w_in"].at[0], D_MODEL, BF16_ROWS)
        _copy_rows(w_out, w["w_out"].at[0], D_MODEL, BF16_ROWS)
        r_i = lax.broadcasted_iota(jnp.int32, (CHUNK, CHUNK), 0)
        c_i = lax.broadcasted_iota(jnp.int32, (CHUNK, CHUNK), 1)
        keep = c_i <= r_i
        if seq_len < CHUNK:
            keep = keep & (r_i // seq_len == c_i // seq_len)
            in_corner = lax.broadcasted_iota(jnp.int32, (seq_len, CHUNK), 1) < seq_len
        for hd in range(N_HEADS):
            mat = w["sp_w"][0, hd]
            if seq_len < CHUNK:
                corner = jnp.where(in_corner, mat[0:seq_len, :], 0.0)
                lanes = corner
                for q in range(1, CHUNK // seq_len):
                    lanes = lanes + pltpu.roll(corner, q * seq_len, axis=1)
                mat = jnp.concatenate([lanes] * (CHUNK // seq_len), axis=0)
            sp_w[hd] = jnp.where(keep, mat, 0.0).astype(jnp.bfloat16)

    if sample:
        for hist, ref, (n, _, step_major) in zip(hists, state_in, _STATES):
            for h in range(HALVES):
                if step_major:
                    for s in range(n):
                        hist.set_history_step(h, s, n, ref[0, s, :, _half(h)])
                else:
                    for run in range(n_runs):
                        hist.set_history(h, run, ref[0, run, :, _half(h)])
        first_block_cnt = lambda: None
    else:
        @pl.when(j == 0)
        def _():
            for hist in hists:
                hist.ref[:, 0:hist.past, :] = jnp.zeros((HALVES, hist.past, LANES), jnp.float32)

        @pl.when(j > 0)
        def _():
            for hist in hists:
                hist.ref[:, 0:hist.past, :] = hist.ref[:, seq_len:seq_len + hist.past, :]

        def first_block_cnt():
            return j * seq_len + lax.broadcasted_iota(jnp.int32, (ROWS, LANES), 0) + 1

    layer = pl.program_id(0) if sample else n_prev

    def of_layer(get):
        if isinstance(layer, int):
            return get(layer)
        val = get(0)
        for k in range(1, w["pre_g"].shape[0]):
            val = jnp.where(layer == k, get(k), val)
        return val

    rows = {}
    for i, name in enumerate(_ROW_VECTORS):
        width = w[name].shape[1]
        vec_l[i:i + 1, 0:width] = of_layer(lambda k, name=name: w[name][k:k + 1, :])
        rows[name] = _Row(vec_l, i, width)
    ca_w_l[...] = of_layer(lambda k: w["ca_w"][:, k, :])
    cb_w_l[...] = of_layer(lambda k: w["cb_w"][:, k, :])
    block = lambda name: w[name].at[0]
    p = types.SimpleNamespace(
        w_in=w_in, w_out=w_out, sp_w=sp_w, ca_w=ca_w_l, cb_w=cb_w_l, sp_b=block("sp_b"), pl_w=block("pl_w"),
        **rows)

    def chunk_bias(h):
        steps = min(seq_len, CHUNK)
        return jnp.concatenate([p.sp_b[0:steps, _half(h)]] * (CHUNK // steps), axis=0)

    t = types.SimpleNamespace(
        x=tokens if sample else x_ref.at[0], ys=(tokens, y_ref) if sample else (y_ref.at[0],),
        hist_a=hists[0], hist_b=hists[1], hist_d=hists[2], proj=proj, vn=vn_ref, mix=mix,
        n_rows=n_rows, first_block_cnt=first_block_cnt, chunk_bias=chunk_bias, out_cols=slice(0, D_MODEL))
    _phase_in(t, p)
    _phase_mix(t, p)
    _phase_out(t, p)

    def write_stacked():
        for ref, src in zip(stacked_out, carried):
            for k in range(n_prev):
                ref[k] = src[k]
        for hist, ref, (n, _, step_major) in zip(hists, state_out, _STATES):
            for h in range(HALVES):
                if sample and step_major:
                    for s in range(n):
                        ref[n_prev, s, :, _half(h)] = hist.tail_step(h, n_runs, s, n)
                elif step_major:
                    row0 = hist.past + seq_len - n
                    for b in range(ref.shape[2]):
                        @pl.when(pl.program_id(0) == b)
                        def _(b=b, ref=ref, hist=hist, h=h, row0=row0, n=n):
                            for s in range(n):
                                ref[n_prev, s, b:b + 1, _half(h)] = hist.ref[h, row0 + s:row0 + s + 1, :]
                else:
                    for run in range(n_runs):
                        ref[n_prev, run, :, _half(h)] = hist.tail(h, run, n)
        if sample:
            stacked_out[n_state][n_prev] = vn_ref[...]

    if sample:
        write_stacked()
    else:
        pl.when(j == pl.num_programs(1) - 1)(write_stacked)


def _layer_params(w, layer, vec_l, ca_w_l, cb_w_l):
    def of_layer(get):
        if isinstance(layer, int):
            return get(layer)
        val = get(0)
        for k in range(1, w["pre_g"].shape[0]):
            val = jnp.where(layer == k, get(k), val)
        return val

    rows = {}
    for i, name in enumerate(_ROW_VECTORS):
        width = w[name].shape[1]
        vec_l[i:i + 1, 0:width] = of_layer(lambda k, name=name: w[name][k:k + 1, :])
        rows[name] = _Row(vec_l, i, width)
    ca_w_l[...] = of_layer(lambda k: w["ca_w"][:, k, :])
    cb_w_l[...] = of_layer(lambda k: w["cb_w"][:, k, :])
    return dict(rows, ca_w=ca_w_l, cb_w=cb_w_l, sp_b=w["sp_b"].at[0], pl_w=w["pl_w"].at[0])


def _prompt_kernel(*refs, n_prev, n_tiles, tiles_per_seq, tile):
    n_state, n_w = len(_STATES), len(_WEIGHTS)
    n_in = 2 + n_w + (n_state if n_prev else 0)
    x_prev, x_next = refs[0].at[0], refs[1].at[0]
    w = dict(zip(_WEIGHTS, refs[2:2 + n_w]))
    carried = refs[2 + n_w:n_in]
    y_ref, state_out = refs[n_in].at[0], refs[n_in + 1:n_in + 1 + n_state]
    sp_w, vec_l, ca_w_l, cb_w_l, ext_a, ext_b, ext_d, proj, vn_ref, mix, hbuf = refs[n_in + 1 + n_state:]
    hists = [_History(ext, past, tile) for ext, (_, past, _) in zip((ext_a, ext_b, ext_d), _STATES)]
    s = pl.program_id(0)
    step_in_seq = lax.rem(s, tiles_per_seq)
    p = types.SimpleNamespace(w_in=w["w_in"].at[0], w_out=w["w_out"].at[0], sp_w=sp_w,
                              **_layer_params(w, n_prev, vec_l, ca_w_l, cb_w_l))
    out_cols = slice(N_IN - D_MODEL, N_IN)
    out = proj.at[:, out_cols]

    @pl.when(s == 0)
    def _():
        keep = (lax.broadcasted_iota(jnp.int32, (CHUNK, CHUNK), 1)
                <= lax.broadcasted_iota(jnp.int32, (CHUNK, CHUNK), 0))
        for hd in range(N_HEADS):
            sp_w[hd] = jnp.where(keep, w["sp_w"][0, hd], 0.0).astype(jnp.bfloat16)
        _prenorm(x_prev, hbuf, p)
        out[...] = jnp.zeros((tile, D_MODEL), jnp.float32)

    def write_states():
        for ref, src in zip(state_out, carried):
            for k in range(n_prev):
                ref[k] = src[k]
        for hist, ref, (n, _, step_major) in zip(hists, state_out, _STATES):
            row0 = hist.past + tile - n
            for b in range(n_tiles // tiles_per_seq):
                @pl.when(s == b * tiles_per_seq + tiles_per_seq - 1)
                def _(hist=hist, ref=ref, n=n, step_major=step_major, row0=row0, b=b):
                    for h in range(HALVES):
                        if step_major:
                            for i in range(n):
                                ref[n_prev, i, b:b + 1, _half(h)] = hist.ref[h, row0 + i:row0 + i + 1, :]
                        else:
                            ref[n_prev, b, :, _half(h)] = hist.tail(h, 0, n)

    @pl.when(s < n_tiles)
    def _():
        @pl.when(step_in_seq == 0)
        def _():
            for hist in hists:
                hist.ref[:, 0:hist.past, :] = jnp.zeros((HALVES, hist.past, LANES), jnp.float32)

        @pl.when(step_in_seq > 0)
        def _():
            for hist in hists:
                hist.ref[:, 0:hist.past, :] = hist.ref[:, tile:tile + hist.past, :]

        def first_block_cnt():
            return step_in_seq * tile + lax.broadcasted_iota(jnp.int32, (ROWS, LANES), 0) + 1

        t = types.SimpleNamespace(
            hist_a=hists[0], hist_b=hists[1], hist_d=hists[2], proj=proj, vn=vn_ref, mix=mix, n_rows=tile,
            first_block_cnt=first_block_cnt, chunk_bias=lambda h: p.sp_b[:, _half(h)], out_cols=out_cols)
        _postnorm(x_prev, (y_ref,), out, p)
        proj[...] = jnp.dot(hbuf[...], p.w_in[...], preferred_element_type=jnp.float32)
        _phase_mix(t, p)
        _prenorm(x_next, hbuf, p)

        pl.when(step_in_seq == tiles_per_seq - 1)(write_states)

    @pl.when(s == n_tiles)
    def _():
        _postnorm(x_prev, (y_ref,), out, p)


def _layer_spec(arr, l, single_buffer=False):
    shape = (1,) + arr.shape[1:]
    index_map = lambda *g: (g[0] if l is None else l,) + (0,) * (len(shape) - 1)
    if single_buffer:
        return pl.BlockSpec(shape, index_map, pipeline_mode=pl.Buffered(1))
    return pl.BlockSpec(shape, index_map)


def _small_scratch():
    return [
        pltpu.VMEM((len(_ROW_VECTORS), D_MODEL), jnp.float32),
        pltpu.VMEM((CONV_A_WIDTH, GROUP), jnp.float32),
        pltpu.VMEM((CONV_B_WIDTH, GROUP), jnp.float32),
    ]


def _tile_scratch(n_runs, seq_len):
    n_rows = n_runs * seq_len
    return [
        pltpu.VMEM((HALVES, n_runs * (past + seq_len), LANES), jnp.float32) for _, past, _ in _STATES
    ] + [
        pltpu.VMEM((n_rows, N_IN), jnp.float32),
        pltpu.VMEM((n_rows, GROUP), jnp.float32),
        pltpu.VMEM((n_rows, D_MODEL), jnp.bfloat16),
    ]


def _scratch(n_runs, seq_len):
    return [
        pltpu.VMEM((D_MODEL, N_IN), jnp.bfloat16),
        pltpu.VMEM((D_MODEL, D_MODEL), jnp.bfloat16),
        pltpu.VMEM((N_HEADS, CHUNK, CHUNK), jnp.bfloat16),
    ] + _small_scratch() + _tile_scratch(n_runs, seq_len)


_LAYER_BLOCKS = ("w_in", "w_out", "sp_w", "sp_b", "pl_w")
_ROW_VECTORS = ("pre_g", "post_g", "ca_b", "lna_g", "lna_b", "lnc_g", "lnc_b", "pl_s")


def _weight_args(w, l):
    def spec(n):
        if n in _LAYER_BLOCKS:
            return _layer_spec(w[n], l, single_buffer=n in ("w_in", "w_out"))
        return pl.BlockSpec(w[n].shape, lambda *g: (0,) * w[n].ndim)

    return [spec(n) for n in _WEIGHTS], [w[n] for n in _WEIGHTS]


def _stacked_specs(n_layers, tails):
    def spec(block_tail, tail_index):
        return pl.BlockSpec((n_layers,) + block_tail, lambda *g: (0,) + tail_index(*g))

    return [spec(*tail) for tail in tails]


def _prompt_layer(x, w, l, prev):
    batch, seq, _ = x.shape
    tile = PROMPT_TILE
    assert seq % tile == 0 and tile % CHUNK == 0 and tile >= PAST_A
    tiles_per_seq = seq // tile
    n_tiles = batch * tiles_per_seq

    def tile_spec(shift):
        def index(s):
            i = jnp.clip(s + shift, 0, n_tiles - 1)
            return (i // tiles_per_seq, i % tiles_per_seq, 0)
        return pl.BlockSpec((1, tile, D_MODEL), index)

    shapes = [(n, batch, GROUP) if step_major else (batch, n, GROUP) for n, _, step_major in _STATES]
    tails = [(shape, lambda s: (0, 0, 0)) for shape in shapes]
    w_specs, w_args = _weight_args(w, l)
    return pl.pallas_call(
        functools.partial(_prompt_kernel, n_prev=l, n_tiles=n_tiles, tiles_per_seq=tiles_per_seq, tile=tile),
        grid=(n_tiles + 1,),
        in_specs=[tile_spec(-1), tile_spec(1)] + w_specs + (_stacked_specs(l, tails) if prev else []),
        out_specs=[tile_spec(-1)] + _stacked_specs(l + 1, tails),
        out_shape=[jax.ShapeDtypeStruct(x.shape, x.dtype)]
        + [jax.ShapeDtypeStruct((l + 1,) + shape, x.dtype) for shape in shapes],
        scratch_shapes=[pltpu.VMEM((N_HEADS, CHUNK, CHUNK), jnp.bfloat16)] + _small_scratch()
        + _tile_scratch(1, tile) + [pltpu.VMEM((tile, D_MODEL), jnp.bfloat16)],
        compiler_params=pltpu.CompilerParams(dimension_semantics=("arbitrary",), vmem_limit_bytes=VMEM_LIMIT),
        name="prompt_layer",
    )(x, x, *w_args, *prev)


def _sample_layers(x2d, states, w, n_steps):
    batch = x2d.shape[0] // n_steps
    depth = states[0].shape[0]
    seqs = SAMPLE_SEQS
    assert batch % seqs == 0 and ROWS % n_steps == 0 and (seqs * n_steps) % CHUNK == 0 and CHUNK % n_steps == 0
    rows = seqs * n_steps
    state_specs = [pl.BlockSpec((1, n, seqs, GROUP), lambda l, i: (l, 0, i, 0)) if step_major
                   else pl.BlockSpec((1, seqs, n, GROUP), lambda l, i: (l, i, 0, 0))
                   for n, _, step_major in _STATES]
    tok = pl.BlockSpec((rows, D_MODEL), lambda l, i: (jnp.where(l == 0, i, batch // seqs - 1), 0))
    tok_out = pl.BlockSpec((rows, D_MODEL), lambda l, i: (jnp.where(l == depth - 1, i, 0), 0))
    w_specs, w_args = _weight_args(w, None)
    return pl.pallas_call(
        functools.partial(_layer_kernel, sample=True, n_runs=seqs, seq_len=n_steps, n_prev=0),
        grid=(depth, batch // seqs),
        in_specs=[tok] + state_specs + w_specs,
        out_specs=[tok_out] + state_specs + [pl.BlockSpec((1, rows, GROUP), lambda l, i: (l, i, 0))],
        out_shape=[jax.ShapeDtypeStruct(x2d.shape, x2d.dtype)]
        + [jax.ShapeDtypeStruct(s.shape, x2d.dtype) for s in states]
        + [jax.ShapeDtypeStruct((depth, batch * n_steps, GROUP), x2d.dtype)],
        scratch_shapes=_scratch(seqs, n_steps) + [pltpu.VMEM((batch // seqs, rows, D_MODEL), jnp.float32)],
        compiler_params=pltpu.CompilerParams(
            dimension_semantics=("arbitrary", "arbitrary"), vmem_limit_bytes=VMEM_LIMIT),
        name="sample_layers",
    )(x2d, *states, *w_args)


def _prepare_weights(pre_norm_g, w_in, conv_a_w, conv_a_b, ln_a_g, ln_a_b, conv_b_w, ln_c_g, ln_c_b,
                     spatial_w, spatial_b, pool_w, pool_scale, w_out, post_norm_g):
    depth = w_in.shape[0]
    eye = jnp.eye(N_HEADS, dtype=pool_w.dtype)
    pool_bd = jnp.einsum("lgcd,gh->lgchd", pool_w, eye).reshape(depth, GROUP, GROUP)
    return {
        "pre_g": pre_norm_g, "w_in": w_in.astype(jnp.bfloat16),
        "ca_w": jnp.swapaxes(conv_a_w, 0, 1), "ca_b": conv_a_b, "lna_g": ln_a_g, "lna_b": ln_a_b,
        "cb_w": jnp.swapaxes(conv_b_w, 0, 1), "lnc_g": ln_c_g, "lnc_b": ln_c_b,
        "sp_w": spatial_w, "sp_b": jnp.repeat(spatial_b.transpose(0, 2, 1), HEAD_DIM, axis=2),
        "pl_w": pool_bd.astype(jnp.bfloat16), "pl_s": pool_scale,
        "w_out": w_out.astype(jnp.bfloat16), "post_g": post_norm_g,
    }


def kernel(x_prompt, x_sample, state_conv_a, state_conv_b, state_pool, pre_norm_g, w_in, conv_a_w, conv_a_b, ln_a_g, ln_a_b, conv_b_w, ln_c_g, ln_c_b, spatial_w, spatial_b, pool_w, pool_scale, w_out, post_norm_g):
    depth = w_in.shape[0]
    dec_batch, n_steps, _ = x_sample.shape
    w = _prepare_weights(pre_norm_g, w_in, conv_a_w, conv_a_b, ln_a_g, ln_a_b, conv_b_w, ln_c_g,
                         ln_c_b, spatial_w, spatial_b, pool_w, pool_scale, w_out, post_norm_g)
    step_major = lambda v, flag: jnp.swapaxes(v, 1, 2) if flag else v
    states = [step_major(s, sm) for s, (_, _, sm) in zip((state_conv_a, state_conv_b, state_pool), _STATES)]
    yp, ys = x_prompt, x_sample.reshape(dec_batch * n_steps, D_MODEL)
    prompt_out = []
    for l in range(depth):
        yp, *prompt_out = _prompt_layer(yp, w, l, prompt_out)
    ys, *sample_out = _sample_layers(ys, states, w, n_steps)
    a_s, b_s, d_s = (step_major(s, sm) for s, (_, _, sm) in zip(sample_out[:3], _STATES))
    v_s = sample_out[3].reshape(depth, dec_batch, n_steps, GROUP)
    a_p, b_p, d_p = (step_major(s, sm) for s, (_, _, sm) in zip(prompt_out, _STATES))
    return (yp, ys.reshape(x_sample.shape), a_p, a_s, b_p, b_s, d_p, d_s, v_s)
```

```python
import functools
import types

import jax
import jax.numpy as jnp
from jax import lax
from jax.experimental import pallas as pl
from jax.experimental.pallas import tpu as pltpu

D_MODEL = 1024
GROUP = 256
HEAD_DIM = 64
N_HEADS = 4
CONV_A_WIDTH = 31
CONV_B_WIDTH = 3
CHUNK = 128
POOL_BUF = 15
EPS = 1e-6
LANES = 128
BF16_ROWS = 16
HALVES = GROUP // LANES

(A_VAL, A_GATE, Z_A, B_B, B_C, B_X, Z_B, C_U, C_V, Z_C, D_X, Z_D) = range(12)
N_IN = 12 * GROUP
SPATIAL_OUT = C_V
POOL_OUT = D_X

PAST_A = 32
PAST_B = 8
PAST_D = 16

PROMPT_TILE = 1024
SAMPLE_SEQS = 64
ROWS = 32

VMEM_LIMIT = 58 * 1024 * 1024


def _sigmoid(v):
    return 1.0 / (1.0 + jnp.exp(-v))


def _silu(v):
    return v * _sigmoid(v)


def _rowsum(vs):
    tot = vs[0]
    for v in vs[1:]:
        tot = tot + v
    return jnp.sum(tot, axis=-1, keepdims=True)


def _half(h):
    return slice(h * LANES, (h + 1) * LANES)


def _slot(slot, h):
    return slice(slot * GROUP + h * LANES, slot * GROUP + (h + 1) * LANES)


def _layernorm2(v0, v1, g_ref, b_ref):
    mu = _rowsum([v0, v1]) * (1.0 / GROUP)
    d0, d1 = v0 - mu, v1 - mu
    var = _rowsum([d0 * d0, d1 * d1]) * (1.0 / GROUP)
    inv = lax.rsqrt(var + EPS)
    return (d0 * inv * g_ref[:, _half(0)] + b_ref[:, _half(0)],
            d1 * inv * g_ref[:, _half(1)] + b_ref[:, _half(1)])


def _copy_rows(dst, src, n_rows, chunk):
    def body(i, carry):
        r = pl.multiple_of(i * chunk, chunk)
        dst[pl.ds(r, chunk), :] = src[pl.ds(r, chunk), :].astype(dst.dtype)
        return carry
    lax.fori_loop(0, n_rows // chunk, body, 0)


class _Row:
    def __init__(self, ref, i, width):
        self.ref, self.i, self.width = ref, i, width

    def __getitem__(self, idx):
        lanes = slice(0, self.width) if idx is Ellipsis else idx[1]
        return self.ref[self.i:self.i + 1, lanes]


class _History:
    def __init__(self, ref, past, seq_len):
        self.ref, self.past, self.seq_len = ref, past, seq_len
        self.piece = min(seq_len, ROWS)

    def _offsets(self, r0):
        for i in range(ROWS // self.piece):
            run, step = divmod(r0 + i * self.piece, self.seq_len)
            yield i, run * (self.past + self.seq_len) + self.past + step

    def window(self, h, r0, back, extra=0):
        assert extra == 0 or self.piece == ROWS
        parts = [self.ref[h, off - back - extra:off - back + self.piece, :] for _, off in self._offsets(r0)]
        return parts[0] if len(parts) == 1 else jnp.concatenate(parts, axis=0)

    def store(self, h, r0, value):
        for i, off in self._offsets(r0):
            self.ref[h, off:off + self.piece, :] = value[i * self.piece:(i + 1) * self.piece]

    def tail(self, h, run, n):
        end = (run + 1) * (self.past + self.seq_len)
        return self.ref[h, end - n:end, :]

    def set_history(self, h, run, value):
        start = run * (self.past + self.seq_len) + self.past
        self.ref[h, start - value.shape[0]:start, :] = value

    def _across_runs(self, row, n_runs):
        return pl.ds(row, n_runs, stride=self.past + self.seq_len)

    def tail_step(self, h, n_runs, j, n):
        return self.ref.at[h][self._across_runs(self.past + self.seq_len - n + j, n_runs), :]

    def set_history_step(self, h, j, n, value):
        self.ref.at[h][self._across_runs(self.past - n + j, value.shape[0]), :] = value


def _phase_in(t, p):
    for r0 in range(0, t.n_rows, ROWS):
        xb = t.x[r0:r0 + ROWS, :]
        ms = jnp.sum(xb * xb, axis=-1, keepdims=True) * (1.0 / D_MODEL)
        hb = xb * lax.rsqrt(ms + EPS) * p.pre_g[...]
        t.mix[r0:r0 + ROWS, :] = hb.astype(jnp.bfloat16)
    t.proj[...] = jnp.dot(t.mix[...], p.w_in[...], preferred_element_type=jnp.float32)


def _phase_mix(t, p):
    lo = lax.broadcasted_iota(jnp.int32, (ROWS, LANES), 1) < HEAD_DIM
    inv_w = [jnp.where(lo, 1.0 / 2, 1.0 / 4), jnp.where(lo, 1.0 / 8, 1.0 / 16)]
    win = [jnp.where(lo, 2, 4), jnp.where(lo, 8, 16)]
    hist_a, hist_b, hist_d = t.hist_a, t.hist_b, t.hist_d

    def pj(r0, slot, h):
        return t.proj[r0:r0 + ROWS, _slot(slot, h)]

    for r0 in range(0, t.n_rows, ROWS):
        for h in range(HALVES):
            hist_a.store(h, r0, pj(r0, A_VAL, h) * _sigmoid(pj(r0, A_GATE, h)))
            hist_b.store(h, r0, pj(r0, B_C, h) * pj(r0, B_X, h))
            hist_d.store(h, r0, pj(r0, D_X, h))
        v0, v1 = _layernorm2(pj(r0, C_V, 0), pj(r0, C_V, 1), p.lnc_g, p.lnc_b)
        t.vn[r0:r0 + ROWS, _half(0)] = v0
        t.vn[r0:r0 + ROWS, _half(1)] = v1

        a2 = hist_d.window(0, r0, 0) + hist_d.window(0, r0, 1)
        a4 = a2 + (hist_d.window(0, r0, 2) + hist_d.window(0, r0, 3))
        sums = [jnp.where(lo, a2, a4)]
        if hist_d.piece == ROWS:
            s8 = hist_d.window(1, r0, 0, extra=8)
            for j in range(1, 8):
                s8 = s8 + hist_d.window(1, r0, j, extra=8)
            s8_cur, s8_prev = s8[8:], s8[:ROWS]
        else:
            s8_cur, s8_prev = hist_d.window(1, r0, 0), hist_d.window(1, r0, 8)
            for j in range(1, 8):
                s8_cur = s8_cur + hist_d.window(1, r0, j)
                s8_prev = s8_prev + hist_d.window(1, r0, 8 + j)
        sums.append(jnp.where(lo, s8_cur, s8_cur + s8_prev))
        cnt = t.first_block_cnt() if r0 == 0 else None
        for h in range(HALVES):
            if cnt is None:
                mean = sums[h] * inv_w[h]
            else:
                mean = sums[h] / jnp.minimum(cnt, win[h]).astype(jnp.float32)
            t.mix[r0:r0 + ROWS, _slot(3, h)] = (mean - hist_d.window(h, r0, 0)).astype(jnp.bfloat16)

    lo_chunk = lax.broadcasted_iota(jnp.int32, (CHUNK, LANES), 1) < HEAD_DIM
    for c0 in range(0, t.n_rows, CHUNK):
        for h in range(HALVES):
            vb = t.vn[c0:c0 + CHUNK, _half(h)].astype(jnp.bfloat16)
            s_even = jnp.dot(p.sp_w[2 * h], vb, preferred_element_type=jnp.float32)
            s_odd = jnp.dot(p.sp_w[2 * h + 1], vb, preferred_element_type=jnp.float32)
            t.proj[c0:c0 + CHUNK, _slot(SPATIAL_OUT, h)] = (
                jnp.where(lo_chunk, s_even, s_odd) + t.chunk_bias(h))
    t.proj[:, POOL_OUT * GROUP:(POOL_OUT + 1) * GROUP] = jnp.dot(
        t.mix[:, 3 * GROUP:4 * GROUP], p.pl_w[...], preferred_element_type=jnp.float32)

    for r0 in range(0, t.n_rows, ROWS):
        _mix_block(t, p, r0)
    t.proj[:, 0:D_MODEL] = jnp.dot(t.mix[...], p.w_out[...], preferred_element_type=jnp.float32)


def _mix_block(t, p, r0):
    hist_a, hist_b = t.hist_a, t.hist_b

    def pj(r0, slot, h):
        return t.proj[r0:r0 + ROWS, _slot(slot, h)]

    ya = []
    for h in range(HALVES):
        acc = jnp.broadcast_to(p.ca_b[:, _half(h)], (ROWS, LANES))
        for k in range(CONV_A_WIDTH):
            acc = acc + p.ca_w[k:k + 1, _half(h)] * hist_a.window(h, r0, CONV_A_WIDTH - 1 - k)
        ya.append(acc)
    na = _layernorm2(ya[0], ya[1], p.lna_g, p.lna_b)
    for h in range(HALVES):
        t.mix[r0:r0 + ROWS, _slot(0, h)] = (_silu(na[h]) * _silu(pj(r0, Z_A, h))).astype(jnp.bfloat16)

    for h in range(HALVES):
        acc = None
        for k in range(CONV_B_WIDTH):
            term = p.cb_w[k:k + 1, _half(h)] * hist_b.window(h, r0, CONV_B_WIDTH - 1 - k)
            acc = term if acc is None else acc + term
        yb = pj(r0, B_B, h) * acc * _silu(pj(r0, Z_B, h))
        t.mix[r0:r0 + ROWS, _slot(1, h)] = yb.astype(jnp.bfloat16)

    for h in range(HALVES):
        yc = pj(r0, C_U, h) * pj(r0, SPATIAL_OUT, h) * _silu(pj(r0, Z_C, h))
        t.mix[r0:r0 + ROWS, _slot(2, h)] = yc.astype(jnp.bfloat16)
        yd = pj(r0, POOL_OUT, h) * p.pl_s[:, _half(h)] * _silu(pj(r0, Z_D, h))
        t.mix[r0:r0 + ROWS, _slot(3, h)] = yd.astype(jnp.bfloat16)


def _phase_out(t, p):
    for r0 in range(0, t.n_rows, ROWS):
        ob = t.proj[r0:r0 + ROWS, 0:D_MODEL]
        ms = jnp.sum(ob * ob, axis=-1, keepdims=True) * (1.0 / D_MODEL)
        yb = t.x[r0:r0 + ROWS, :] + ob * lax.rsqrt(ms + EPS) * p.post_g[...]
        for y in t.ys:
            y[r0:r0 + ROWS, :] = yb


_WEIGHTS = ("pre_g", "w_in", "ca_w", "ca_b", "lna_g", "lna_b", "cb_w", "lnc_g", "lnc_b", "sp_w", "sp_b",
            "pl_w", "pl_s", "w_out", "post_g")
_STATES = ((CONV_A_WIDTH - 1, PAST_A, True), (CONV_B_WIDTH - 1, PAST_B, False), (POOL_BUF, PAST_D, True))


def _layer_kernel(*refs, sample, n_runs, seq_len, n_prev):
    n_state, n_w = len(_STATES), len(_WEIGHTS)
    n_stacked = n_state + (1 if sample else 0)
    n_carried = n_stacked if n_prev else 0
    n_in = 1 + (n_state if sample else 0) + n_w + n_carried
    n_out = 1 + n_stacked
    x_ref, state_in = refs[0], refs[1:n_in - n_w - n_carried]
    w = dict(zip(_WEIGHTS, refs[n_in - n_w - n_carried:n_in - n_carried]))
    carried = refs[n_in - n_carried:n_in]
    y_ref, stacked_out = refs[n_in], refs[n_in + 1:n_in + n_out]
    state_out = stacked_out[:n_state]
    scratch = refs[n_in + n_out:]
    w_in, w_out, sp_w, vec_l, ca_w_l, cb_w_l, ext_a, ext_b, ext_d, proj, vn_ref, mix = scratch[:12]
    n_rows = n_runs * seq_len
    hists = [_History(ext, past, seq_len) for ext, (_, past, _) in zip((ext_a, ext_b, ext_d), _STATES)]

    if sample:
        first = pl.program_id(1) == 0
        tokens = scratch[12].at[pl.program_id(1)]

        @pl.when(pl.program_id(0) == 0)
        def _():
            tokens[...] = x_ref[...]
    else:
        j = pl.program_id(1)
        first = (pl.program_id(0) == 0) & (j == 0)

    @pl.when(first)
    def _():
        _copy_rows(w_in, w["w_in"].at[0], D_MODEL, BF16_ROWS)
        _copy_rows(w_out, w["w_out"].at[0], D_MODEL, BF16_ROWS)
        r_i = lax.broadcasted_iota(jnp.int32, (CHUNK, CHUNK), 0)
        c_i = lax.broadcasted_iota(jnp.int32, (CHUNK, CHUNK), 1)
        keep = c_i <= r_i
        if seq_len < CHUNK:
            keep = keep & (r_i // seq_len == c_i // seq_len)
            in_corner = lax.broadcasted_iota(jnp.int32, (seq_len, CHUNK), 1) < seq_len
        for hd in range(N_HEADS):
            mat = w["sp_w"][0, hd]
            if seq_len < CHUNK:
                corner = jnp.where(in_corner, mat[0:seq_len, :], 0.0)
                lanes = corner
                for q in range(1, CHUNK // seq_len):
                    lanes = lanes + pltpu.roll(corner, q * seq_len, axis=1)
                mat = jnp.concatenate([lanes] * (CHUNK // seq_len), axis=0)
            sp_w[hd] = jnp.where(keep, mat, 0.0).astype(jnp.bfloat16)

    if sample:
        for hist, ref, (n, _, step_major) in zip(hists, state_in, _STATES):
            for h in range(HALVES):
                if step_major:
                    for s in range(n):
                        hist.set_history_step(h, s, n, ref[0, s, :, _half(h)])
                else:
                    for run in range(n_runs):
                        hist.set_history(h, run, ref[0, run, :, _half(h)])
        first_block_cnt = lambda: None
    else:
        @pl.when(j == 0)
        def _():
            for hist in hists:
                hist.ref[:, 0:hist.past, :] = jnp.zeros((HALVES, hist.past, LANES), jnp.float32)

        @pl.when(j > 0)
        def _():
            for hist in hists:
                hist.ref[:, 0:hist.past, :] = hist.ref[:, seq_len:seq_len + hist.past, :]

        def first_block_cnt():
            return j * seq_len + lax.broadcasted_iota(jnp.int32, (ROWS, LANES), 0) + 1

    layer = pl.program_id(0) if sample else n_prev
    p = types.SimpleNamespace(w_in=w_in, w_out=w_out, sp_w=sp_w, **_layer_params(w, layer, vec_l, ca_w_l, cb_w_l))

    def chunk_bias(h):
        steps = min(seq_len, CHUNK)
        return jnp.concatenate([p.sp_b[0:steps, _half(h)]] * (CHUNK // steps), axis=0)

    t = types.SimpleNamespace(
        x=tokens if sample else x_ref.at[0], ys=(tokens, y_ref) if sample else (y_ref.at[0],),
        hist_a=hists[0], hist_b=hists[1], hist_d=hists[2], proj=proj, vn=vn_ref, mix=mix,
        n_rows=n_rows, first_block_cnt=first_block_cnt, chunk_bias=chunk_bias)
    _phase_in(t, p)
    _phase_mix(t, p)
    _phase_out(t, p)

    def write_stacked():
        for ref, src in zip(stacked_out, carried):
            for k in range(n_prev):
                ref[k] = src[k]
        for hist, ref, (n, _, step_major) in zip(hists, state_out, _STATES):
            for h in range(HALVES):
                if sample and step_major:
                    for s in range(n):
                        ref[n_prev, s, :, _half(h)] = hist.tail_step(h, n_runs, s, n)
                elif step_major:
                    row0 = hist.past + seq_len - n
                    for b in range(ref.shape[2]):
                        @pl.when(pl.program_id(0) == b)
                        def _(b=b, ref=ref, hist=hist, h=h, row0=row0, n=n):
                            for s in range(n):
                                ref[n_prev, s, b:b + 1, _half(h)] = hist.ref[h, row0 + s:row0 + s + 1, :]
                else:
                    for run in range(n_runs):
                        ref[n_prev, run, :, _half(h)] = hist.tail(h, run, n)
        if sample:
            stacked_out[n_state][n_prev] = vn_ref[...]

    if sample:
        write_stacked()
    else:
        pl.when(j == pl.num_programs(1) - 1)(write_stacked)


def _layer_params(w, layer, vec_l, ca_w_l, cb_w_l):
    def of_layer(get):
        if isinstance(layer, int):
            return get(layer)
        val = get(0)
        for k in range(1, w["pre_g"].shape[0]):
            val = jnp.where(layer == k, get(k), val)
        return val

    rows = {}
    for i, name in enumerate(_ROW_VECTORS):
        width = w[name].shape[1]
        vec_l[i:i + 1, 0:width] = of_layer(lambda k, name=name: w[name][k:k + 1, :])
        rows[name] = _Row(vec_l, i, width)
    ca_w_l[...] = of_layer(lambda k: w["ca_w"][:, k, :])
    cb_w_l[...] = of_layer(lambda k: w["cb_w"][:, k, :])
    return dict(rows, ca_w=ca_w_l, cb_w=cb_w_l, sp_b=w["sp_b"].at[0], pl_w=w["pl_w"].at[0])


def _layer_spec(arr, l, single_buffer=False):
    shape = (1,) + arr.shape[1:]
    index_map = lambda *g: (g[0] if l is None else l,) + (0,) * (len(shape) - 1)
    if single_buffer:
        return pl.BlockSpec(shape, index_map, pipeline_mode=pl.Buffered(1))
    return pl.BlockSpec(shape, index_map)


def _small_scratch():
    return [
        pltpu.VMEM((len(_ROW_VECTORS), D_MODEL), jnp.float32),
        pltpu.VMEM((CONV_A_WIDTH, GROUP), jnp.float32),
        pltpu.VMEM((CONV_B_WIDTH, GROUP), jnp.float32),
    ]


def _tile_scratch(n_runs, seq_len):
    n_rows = n_runs * seq_len
    return [
        pltpu.VMEM((HALVES, n_runs * (past + seq_len), LANES), jnp.float32) for _, past, _ in _STATES
    ] + [
        pltpu.VMEM((n_rows, N_IN), jnp.float32),
        pltpu.VMEM((n_rows, GROUP), jnp.float32),
        pltpu.VMEM((n_rows, D_MODEL), jnp.bfloat16),
    ]


def _scratch(n_runs, seq_len):
    return [
        pltpu.VMEM((D_MODEL, N_IN), jnp.bfloat16),
        pltpu.VMEM((D_MODEL, D_MODEL), jnp.bfloat16),
        pltpu.VMEM((N_HEADS, CHUNK, CHUNK), jnp.bfloat16),
    ] + _small_scratch() + _tile_scratch(n_runs, seq_len)


_LAYER_BLOCKS = ("w_in", "w_out", "sp_w", "sp_b", "pl_w")
_ROW_VECTORS = ("pre_g", "post_g", "ca_b", "lna_g", "lna_b", "lnc_g", "lnc_b", "pl_s")


def _weight_args(w, l):
    def spec(n):
        if n in _LAYER_BLOCKS:
            return _layer_spec(w[n], l, single_buffer=n in ("w_in", "w_out"))
        return pl.BlockSpec(w[n].shape, lambda *g: (0,) * w[n].ndim)

    return [spec(n) for n in _WEIGHTS], [w[n] for n in _WEIGHTS]


def _stacked_specs(n_layers, tails):
    def spec(block_tail, tail_index):
        return pl.BlockSpec((n_layers,) + block_tail, lambda *g: (0,) + tail_index(*g))

    return [spec(*tail) for tail in tails]


def _prompt_layer(x, w, l, prev):
    batch, seq, _ = x.shape
    tile = PROMPT_TILE
    assert seq % tile == 0 and tile % CHUNK == 0 and tile >= PAST_A
    tok = pl.BlockSpec((1, tile, D_MODEL), lambda b, j: (b, j, 0))
    tails = [((n, batch, GROUP), lambda b, j: (0, 0, 0)) if step_major else ((1, n, GROUP), lambda b, j: (b, 0, 0))
             for n, _, step_major in _STATES]
    w_specs, w_args = _weight_args(w, l)
    return pl.pallas_call(
        functools.partial(_layer_kernel, sample=False, n_runs=1, seq_len=tile, n_prev=l),
        grid=(batch, seq // tile),
        in_specs=[tok] + w_specs + (_stacked_specs(l, tails) if prev else []),
        out_specs=[tok] + _stacked_specs(l + 1, tails),
        out_shape=[jax.ShapeDtypeStruct(x.shape, x.dtype)]
        + [jax.ShapeDtypeStruct((l + 1, n, batch, GROUP) if step_major else (l + 1, batch, n, GROUP), x.dtype)
           for n, _, step_major in _STATES],
        scratch_shapes=_scratch(1, tile),
        compiler_params=pltpu.CompilerParams(
            dimension_semantics=("arbitrary", "arbitrary"), vmem_limit_bytes=VMEM_LIMIT),
        name="prompt_layer",
    )(x, *w_args, *prev)


def _sample_layers(x2d, states, w, n_steps):
    batch = x2d.shape[0] // n_steps
    depth = states[0].shape[0]
    seqs = SAMPLE_SEQS
    assert batch % seqs == 0 and ROWS % n_steps == 0 and (seqs * n_steps) % CHUNK == 0 and CHUNK % n_steps == 0
    rows = seqs * n_steps
    state_specs = [pl.BlockSpec((1, n, seqs, GROUP), lambda l, i: (l, 0, i, 0)) if step_major
                   else pl.BlockSpec((1, seqs, n, GROUP), lambda l, i: (l, i, 0, 0))
                   for n, _, step_major in _STATES]
    tok = pl.BlockSpec((rows, D_MODEL), lambda l, i: (jnp.where(l == 0, i, batch // seqs - 1), 0))
    tok_out = pl.BlockSpec((rows, D_MODEL), lambda l, i: (jnp.where(l == depth - 1, i, 0), 0))
    w_specs, w_args = _weight_args(w, None)
    return pl.pallas_call(
        functools.partial(_layer_kernel, sample=True, n_runs=seqs, seq_len=n_steps, n_prev=0),
        grid=(depth, batch // seqs),
        in_specs=[tok] + state_specs + w_specs,
        out_specs=[tok_out] + state_specs + [pl.BlockSpec((1, rows, GROUP), lambda l, i: (l, i, 0))],
        out_shape=[jax.ShapeDtypeStruct(x2d.shape, x2d.dtype)]
        + [jax.ShapeDtypeStruct(s.shape, x2d.dtype) for s in states]
        + [jax.ShapeDtypeStruct((depth, batch * n_steps, GROUP), x2d.dtype)],
        scratch_shapes=_scratch(seqs, n_steps) + [pltpu.VMEM((batch // seqs, rows, D_MODEL), jnp.float32)],
        compiler_params=pltpu.CompilerParams(
            dimension_semantics=("arbitrary", "arbitrary"), vmem_limit_bytes=VMEM_LIMIT),
        name="sample_layers",
    )(x2d, *states, *w_args)


def _prepare_weights(pre_norm_g, w_in, conv_a_w, conv_a_b, ln_a_g, ln_a_b, conv_b_w, ln_c_g, ln_c_b,
                     spatial_w, spatial_b, pool_w, pool_scale, w_out, post_norm_g):
    depth = w_in.shape[0]
    eye = jnp.eye(N_HEADS, dtype=pool_w.dtype)
    pool_bd = jnp.einsum("lgcd,gh->lgchd", pool_w, eye).reshape(depth, GROUP, GROUP)
    return {
        "pre_g": pre_norm_g, "w_in": w_in.astype(jnp.bfloat16),
        "ca_w": jnp.swapaxes(conv_a_w, 0, 1), "ca_b": conv_a_b, "lna_g": ln_a_g, "lna_b": ln_a_b,
        "cb_w": jnp.swapaxes(conv_b_w, 0, 1), "lnc_g": ln_c_g, "lnc_b": ln_c_b,
        "sp_w": spatial_w, "sp_b": jnp.repeat(spatial_b.transpose(0, 2, 1), HEAD_DIM, axis=2),
        "pl_w": pool_bd.astype(jnp.bfloat16), "pl_s": pool_scale,
        "w_out": w_out.astype(jnp.bfloat16), "post_g": post_norm_g,
    }


def kernel(x_prompt, x_sample, state_conv_a, state_conv_b, state_pool, pre_norm_g, w_in, conv_a_w, conv_a_b, ln_a_g, ln_a_b, conv_b_w, ln_c_g, ln_c_b, spatial_w, spatial_b, pool_w, pool_scale, w_out, post_norm_g):
    depth = w_in.shape[0]
    dec_batch, n_steps, _ = x_sample.shape
    w = _prepare_weights(pre_norm_g, w_in, conv_a_w, conv_a_b, ln_a_g, ln_a_b, conv_b_w, ln_c_g,
                         ln_c_b, spatial_w, spatial_b, pool_w, pool_scale, w_out, post_norm_g)
    step_major = lambda v, flag: jnp.swapaxes(v, 1, 2) if flag else v
    states = [step_major(s, sm) for s, (_, _, sm) in zip((state_conv_a, state_conv_b, state_pool), _STATES)]
    yp, ys = x_prompt, x_sample.reshape(dec_batch * n_steps, D_MODEL)
    prompt_out = []
    for l in range(depth):
        yp, *prompt_out = _prompt_layer(yp, w, l, prompt_out)
    ys, *sample_out = _sample_layers(ys, states, w, n_steps)
    a_s, b_s, d_s = (step_major(s, sm) for s, (_, _, sm) in zip(sample_out[:3], _STATES))
    v_s = sample_out[3].reshape(depth, dec_batch, n_steps, GROUP)
    a_p, b_p, d_p = (step_major(s, sm) for s, (_, _, sm) in zip(prompt_out, _STATES))
    return (yp, ys.reshape(x_sample.shape), a_p, a_s, b_p, b_s, d_p, d_s, v_s)
```

```python
import functools
import types

import jax
import jax.numpy as jnp
from jax import lax
from jax.experimental import pallas as pl
from jax.experimental.pallas import tpu as pltpu

D_MODEL = 1024
GROUP = 256
HEAD_DIM = 64
N_HEADS = 4
CONV_A_WIDTH = 31
CONV_B_WIDTH = 3
CHUNK = 128
POOL_BUF = 15
EPS = 1e-6
LANES = 128
HALVES = GROUP // LANES

(A_VAL, A_GATE, Z_A, B_B, B_C, B_X, Z_B, C_U, C_V, Z_C, D_X, Z_D) = range(12)
N_IN = 12 * GROUP
SPATIAL_OUT = C_V
POOL_OUT = D_X

PAST_A = 32
PAST_B = 8
PAST_D = 16

PROMPT_TILE = 1024
SAMPLE_SEQS = 64
ROWS = 32

VMEM_LIMIT = 58 * 1024 * 1024


def _sigmoid(v):
    return 1.0 / (1.0 + jnp.exp(-v))


def _silu(v):
    return v * _sigmoid(v)


def _rowsum(vs):
    tot = vs[0]
    for v in vs[1:]:
        tot = tot + v
    return jnp.sum(tot, axis=-1, keepdims=True)


def _half(h):
    return slice(h * LANES, (h + 1) * LANES)


def _slot(slot, h):
    return slice(slot * GROUP + h * LANES, slot * GROUP + (h + 1) * LANES)


def _layernorm2(v0, v1, g_ref, b_ref):
    mu = _rowsum([v0, v1]) * (1.0 / GROUP)
    d0, d1 = v0 - mu, v1 - mu
    var = _rowsum([d0 * d0, d1 * d1]) * (1.0 / GROUP)
    inv = lax.rsqrt(var + EPS)
    return (d0 * inv * g_ref[:, _half(0)] + b_ref[:, _half(0)],
            d1 * inv * g_ref[:, _half(1)] + b_ref[:, _half(1)])


class _Row:
    def __init__(self, ref, i, width):
        self.ref, self.i, self.width = ref, i, width

    def __getitem__(self, idx):
        lanes = slice(0, self.width) if idx is Ellipsis else idx[1]
        return self.ref[self.i:self.i + 1, lanes]


class _History:
    def __init__(self, ref, past, seq_len):
        self.ref, self.past, self.seq_len = ref, past, seq_len
        self.piece = min(seq_len, ROWS)

    def _offsets(self, r0):
        for i in range(ROWS // self.piece):
            run, step = divmod(r0 + i * self.piece, self.seq_len)
            yield i, run * (self.past + self.seq_len) + self.past + step

    def window(self, h, r0, back, extra=0):
        assert extra == 0 or self.piece == ROWS
        parts = [self.ref[h, off - back - extra:off - back + self.piece, :] for _, off in self._offsets(r0)]
        return parts[0] if len(parts) == 1 else jnp.concatenate(parts, axis=0)

    def store(self, h, r0, value):
        for i, off in self._offsets(r0):
            self.ref[h, off:off + self.piece, :] = value[i * self.piece:(i + 1) * self.piece]

    def tail(self, h, run, n):
        end = (run + 1) * (self.past + self.seq_len)
        return self.ref[h, end - n:end, :]

    def set_history(self, h, run, value):
        start = run * (self.past + self.seq_len) + self.past
        self.ref[h, start - value.shape[0]:start, :] = value

    def _across_runs(self, row, n_runs):
        return pl.ds(row, n_runs, stride=self.past + self.seq_len)

    def tail_step(self, h, n_runs, j, n):
        return self.ref.at[h][self._across_runs(self.past + self.seq_len - n + j, n_runs), :]

    def set_history_step(self, h, j, n, value):
        self.ref.at[h][self._across_runs(self.past - n + j, value.shape[0]), :] = value


def _phase_in(t, p):
    for r0 in range(0, t.n_rows, ROWS):
        xb = t.x[r0:r0 + ROWS, :]
        ms = jnp.sum(xb * xb, axis=-1, keepdims=True) * (1.0 / D_MODEL)
        hb = xb * lax.rsqrt(ms + EPS) * p.pre_g[...]
        t.mix[r0:r0 + ROWS, :] = hb.astype(jnp.bfloat16)
    t.proj[...] = jnp.dot(t.mix[...], p.w_in[...], preferred_element_type=jnp.float32)


def _phase_mix(t, p):
    lo = lax.broadcasted_iota(jnp.int32, (ROWS, LANES), 1) < HEAD_DIM
    inv_w = [jnp.where(lo, 1.0 / 2, 1.0 / 4), jnp.where(lo, 1.0 / 8, 1.0 / 16)]
    win = [jnp.where(lo, 2, 4), jnp.where(lo, 8, 16)]
    hist_a, hist_b, hist_d = t.hist_a, t.hist_b, t.hist_d

    def pj(r0, slot, h):
        return t.proj[r0:r0 + ROWS, _slot(slot, h)]

    for r0 in range(0, t.n_rows, ROWS):
        for h in range(HALVES):
            hist_a.store(h, r0, pj(r0, A_VAL, h) * _sigmoid(pj(r0, A_GATE, h)))
            hist_b.store(h, r0, pj(r0, B_C, h) * pj(r0, B_X, h))
            hist_d.store(h, r0, pj(r0, D_X, h))
        v0, v1 = _layernorm2(pj(r0, C_V, 0), pj(r0, C_V, 1), p.lnc_g, p.lnc_b)
        t.vn[r0:r0 + ROWS, _half(0)] = v0
        t.vn[r0:r0 + ROWS, _half(1)] = v1

        a2 = hist_d.window(0, r0, 0) + hist_d.window(0, r0, 1)
        a4 = a2 + (hist_d.window(0, r0, 2) + hist_d.window(0, r0, 3))
        sums = [jnp.where(lo, a2, a4)]
        if hist_d.piece == ROWS:
            s8 = hist_d.window(1, r0, 0, extra=8)
            for j in range(1, 8):
                s8 = s8 + hist_d.window(1, r0, j, extra=8)
            s8_cur, s8_prev = s8[8:], s8[:ROWS]
        else:
            s8_cur, s8_prev = hist_d.window(1, r0, 0), hist_d.window(1, r0, 8)
            for j in range(1, 8):
                s8_cur = s8_cur + hist_d.window(1, r0, j)
                s8_prev = s8_prev + hist_d.window(1, r0, 8 + j)
        sums.append(jnp.where(lo, s8_cur, s8_cur + s8_prev))
        cnt = t.first_block_cnt() if r0 == 0 else None
        for h in range(HALVES):
            if cnt is None:
                mean = sums[h] * inv_w[h]
            else:
                mean = sums[h] / jnp.minimum(cnt, win[h]).astype(jnp.float32)
            t.mix[r0:r0 + ROWS, _slot(3, h)] = (mean - hist_d.window(h, r0, 0)).astype(jnp.bfloat16)

    lo_chunk = lax.broadcasted_iota(jnp.int32, (CHUNK, LANES), 1) < HEAD_DIM
    for c0 in range(0, t.n_rows, CHUNK):
        for h in range(HALVES):
            vb = t.vn[c0:c0 + CHUNK, _half(h)].astype(jnp.bfloat16)
            s_even = jnp.dot(p.sp_w[2 * h], vb, preferred_element_type=jnp.float32)
            s_odd = jnp.dot(p.sp_w[2 * h + 1], vb, preferred_element_type=jnp.float32)
            t.proj[c0:c0 + CHUNK, _slot(SPATIAL_OUT, h)] = (
                jnp.where(lo_chunk, s_even, s_odd) + t.chunk_bias(h))
    t.proj[:, POOL_OUT * GROUP:(POOL_OUT + 1) * GROUP] = jnp.dot(
        t.mix[:, 3 * GROUP:4 * GROUP], p.pl_w[...], preferred_element_type=jnp.float32)

    for r0 in range(0, t.n_rows, ROWS):
        _mix_block(t, p, r0)
    t.proj[:, 0:D_MODEL] = jnp.dot(t.mix[...], p.w_out[...], preferred_element_type=jnp.float32)


def _mix_block(t, p, r0):
    hist_a, hist_b = t.hist_a, t.hist_b

    def pj(r0, slot, h):
        return t.proj[r0:r0 + ROWS, _slot(slot, h)]

    ya = []
    for h in range(HALVES):
        acc = jnp.broadcast_to(p.ca_b[:, _half(h)], (ROWS, LANES))
        for k in range(CONV_A_WIDTH):
            acc = acc + p.ca_w[k:k + 1, _half(h)] * hist_a.window(h, r0, CONV_A_WIDTH - 1 - k)
        ya.append(acc)
    na = _layernorm2(ya[0], ya[1], p.lna_g, p.lna_b)
    for h in range(HALVES):
        t.mix[r0:r0 + ROWS, _slot(0, h)] = (_silu(na[h]) * _silu(pj(r0, Z_A, h))).astype(jnp.bfloat16)

    for h in range(HALVES):
        acc = None
        for k in range(CONV_B_WIDTH):
            term = p.cb_w[k:k + 1, _half(h)] * hist_b.window(h, r0, CONV_B_WIDTH - 1 - k)
            acc = term if acc is None else acc + term
        yb = pj(r0, B_B, h) * acc * _silu(pj(r0, Z_B, h))
        t.mix[r0:r0 + ROWS, _slot(1, h)] = yb.astype(jnp.bfloat16)

    for h in range(HALVES):
        yc = pj(r0, C_U, h) * pj(r0, SPATIAL_OUT, h) * _silu(pj(r0, Z_C, h))
        t.mix[r0:r0 + ROWS, _slot(2, h)] = yc.astype(jnp.bfloat16)
        yd = pj(r0, POOL_OUT, h) * p.pl_s[:, _half(h)] * _silu(pj(r0, Z_D, h))
        t.mix[r0:r0 + ROWS, _slot(3, h)] = yd.astype(jnp.bfloat16)


def _phase_out(t, p):
    for r0 in range(0, t.n_rows, ROWS):
        ob = t.proj[r0:r0 + ROWS, 0:D_MODEL]
        ms = jnp.sum(ob * ob, axis=-1, keepdims=True) * (1.0 / D_MODEL)
        yb = t.x[r0:r0 + ROWS, :] + ob * lax.rsqrt(ms + EPS) * p.post_g[...]
        for y in t.ys:
            y[r0:r0 + ROWS, :] = yb


_WEIGHTS = ("pre_g", "w_in", "ca_w", "ca_b", "lna_g", "lna_b", "cb_w", "lnc_g", "lnc_b", "sp_w", "sp_b",
            "pl_w", "pl_s", "w_out", "post_g")
_STATES = ((CONV_A_WIDTH - 1, PAST_A, True), (CONV_B_WIDTH - 1, PAST_B, False), (POOL_BUF, PAST_D, True))


def _layer_kernel(*refs, sample, n_runs, seq_len, n_prev):
    n_state, n_w = len(_STATES), len(_WEIGHTS)
    n_stacked = n_state + (1 if sample else 0)
    n_carried = n_stacked if n_prev else 0
    n_in = 1 + (n_state if sample else 0) + n_w + n_carried
    n_out = 1 + n_stacked
    x_ref, state_in = refs[0], refs[1:n_in - n_w - n_carried]
    w = dict(zip(_WEIGHTS, refs[n_in - n_w - n_carried:n_in - n_carried]))
    carried = refs[n_in - n_carried:n_in]
    y_ref, stacked_out = refs[n_in], refs[n_in + 1:n_in + n_out]
    state_out = stacked_out[:n_state]
    scratch = refs[n_in + n_out:]
    sp_w, vec_l, ca_w_l, cb_w_l, ext_a, ext_b, ext_d, proj, vn_ref, mix = scratch[:10]
    n_rows = n_runs * seq_len
    hists = [_History(ext, past, seq_len) for ext, (_, past, _) in zip((ext_a, ext_b, ext_d), _STATES)]

    if sample:
        first = pl.program_id(1) == 0
        tokens = scratch[10].at[pl.program_id(1)]

        @pl.when(pl.program_id(0) == 0)
        def _():
            tokens[...] = x_ref[...]
    else:
        j = pl.program_id(1)
        first = (pl.program_id(0) == 0) & (j == 0)

    @pl.when(first)
    def _():
        r_i = lax.broadcasted_iota(jnp.int32, (CHUNK, CHUNK), 0)
        c_i = lax.broadcasted_iota(jnp.int32, (CHUNK, CHUNK), 1)
        keep = c_i <= r_i
        if seq_len < CHUNK:
            keep = keep & (r_i // seq_len == c_i // seq_len)
            in_corner = lax.broadcasted_iota(jnp.int32, (seq_len, CHUNK), 1) < seq_len
        for hd in range(N_HEADS):
            mat = w["sp_w"][0, hd]
            if seq_len < CHUNK:
                corner = jnp.where(in_corner, mat[0:seq_len, :], 0.0)
                lanes = corner
                for q in range(1, CHUNK // seq_len):
                    lanes = lanes + pltpu.roll(corner, q * seq_len, axis=1)
                mat = jnp.concatenate([lanes] * (CHUNK // seq_len), axis=0)
            sp_w[hd] = jnp.where(keep, mat, 0.0).astype(jnp.bfloat16)

    if sample:
        for hist, ref, (n, _, step_major) in zip(hists, state_in, _STATES):
            for h in range(HALVES):
                if step_major:
                    for s in range(n):
                        hist.set_history_step(h, s, n, ref[0, s, :, _half(h)])
                else:
                    for run in range(n_runs):
                        hist.set_history(h, run, ref[0, run, :, _half(h)])
        first_block_cnt = lambda: None
    else:
        @pl.when(j == 0)
        def _():
            for hist in hists:
                hist.ref[:, 0:hist.past, :] = jnp.zeros((HALVES, hist.past, LANES), jnp.float32)

        @pl.when(j > 0)
        def _():
            for hist in hists:
                hist.ref[:, 0:hist.past, :] = hist.ref[:, seq_len:seq_len + hist.past, :]

        def first_block_cnt():
            return j * seq_len + lax.broadcasted_iota(jnp.int32, (ROWS, LANES), 0) + 1

    layer = pl.program_id(0) if sample else n_prev
    p = types.SimpleNamespace(w_in=w["w_in"].at[0], w_out=w["w_out"].at[0], sp_w=sp_w,
                              **_layer_params(w, layer, vec_l, ca_w_l, cb_w_l))

    def chunk_bias(h):
        steps = min(seq_len, CHUNK)
        return jnp.concatenate([p.sp_b[0:steps, _half(h)]] * (CHUNK // steps), axis=0)

    t = types.SimpleNamespace(
        x=tokens if sample else x_ref.at[0], ys=(tokens, y_ref) if sample else (y_ref.at[0],),
        hist_a=hists[0], hist_b=hists[1], hist_d=hists[2], proj=proj, vn=vn_ref, mix=mix,
        n_rows=n_rows, first_block_cnt=first_block_cnt, chunk_bias=chunk_bias)
    _phase_in(t, p)
    _phase_mix(t, p)
    _phase_out(t, p)

    def write_stacked():
        for ref, src in zip(stacked_out, carried):
            for k in range(n_prev):
                ref[k] = src[k]
        for hist, ref, (n, _, step_major) in zip(hists, state_out, _STATES):
            for h in range(HALVES):
                if sample and step_major:
                    for s in range(n):
                        ref[n_prev, s, :, _half(h)] = hist.tail_step(h, n_runs, s, n)
                elif step_major:
                    row0 = hist.past + seq_len - n
                    for b in range(ref.shape[2]):
                        @pl.when(pl.program_id(0) == b)
                        def _(b=b, ref=ref, hist=hist, h=h, row0=row0, n=n):
                            for s in range(n):
                                ref[n_prev, s, b:b + 1, _half(h)] = hist.ref[h, row0 + s:row0 + s + 1, :]
                else:
                    for run in range(n_runs):
                        ref[n_prev, run, :, _half(h)] = hist.tail(h, run, n)
        if sample:
            stacked_out[n_state][n_prev] = vn_ref[...]

    if sample:
        write_stacked()
    else:
        pl.when(j == pl.num_programs(1) - 1)(write_stacked)


def _layer_params(w, layer, vec_l, ca_w_l, cb_w_l):
    def of_layer(get):
        if isinstance(layer, int):
            return get(layer)
        val = get(0)
        for k in range(1, w["pre_g"].shape[0]):
            val = jnp.where(layer == k, get(k), val)
        return val

    rows = {}
    for i, name in enumerate(_ROW_VECTORS):
        width = w[name].shape[1]
        vec_l[i:i + 1, 0:width] = of_layer(lambda k, name=name: w[name][k:k + 1, :])
        rows[name] = _Row(vec_l, i, width)
    ca_w_l[...] = of_layer(lambda k: w["ca_w"][:, k, :])
    cb_w_l[...] = of_layer(lambda k: w["cb_w"][:, k, :])
    return dict(rows, ca_w=ca_w_l, cb_w=cb_w_l, sp_b=w["sp_b"].at[0], pl_w=w["pl_w"].at[0])


def _layer_spec(arr, l, single_buffer=False):
    shape = (1,) + arr.shape[1:]
    index_map = lambda *g: (g[0] if l is None else l,) + (0,) * (len(shape) - 1)
    if single_buffer:
        return pl.BlockSpec(shape, index_map, pipeline_mode=pl.Buffered(1))
    return pl.BlockSpec(shape, index_map)


def _small_scratch():
    return [
        pltpu.VMEM((len(_ROW_VECTORS), D_MODEL), jnp.float32),
        pltpu.VMEM((CONV_A_WIDTH, GROUP), jnp.float32),
        pltpu.VMEM((CONV_B_WIDTH, GROUP), jnp.float32),
    ]


def _tile_scratch(n_runs, seq_len):
    n_rows = n_runs * seq_len
    return [
        pltpu.VMEM((HALVES, n_runs * (past + seq_len), LANES), jnp.float32) for _, past, _ in _STATES
    ] + [
        pltpu.VMEM((n_rows, N_IN), jnp.float32),
        pltpu.VMEM((n_rows, GROUP), jnp.float32),
        pltpu.VMEM((n_rows, D_MODEL), jnp.bfloat16),
    ]


def _scratch(n_runs, seq_len):
    return ([pltpu.VMEM((N_HEADS, CHUNK, CHUNK), jnp.bfloat16)] + _small_scratch()
            + _tile_scratch(n_runs, seq_len))


_LAYER_BLOCKS = ("w_in", "w_out", "sp_w", "sp_b", "pl_w")
_ROW_VECTORS = ("pre_g", "post_g", "ca_b", "lna_g", "lna_b", "lnc_g", "lnc_b", "pl_s")


def _weight_args(w, l):
    def spec(n):
        if n in _LAYER_BLOCKS:
            return _layer_spec(w[n], l, single_buffer=l is not None and n in ("w_in", "w_out"))
        return pl.BlockSpec(w[n].shape, lambda *g: (0,) * w[n].ndim)

    return [spec(n) for n in _WEIGHTS], [w[n] for n in _WEIGHTS]


def _stacked_specs(n_layers, tails):
    def spec(block_tail, tail_index):
        return pl.BlockSpec((n_layers,) + block_tail, lambda *g: (0,) + tail_index(*g))

    return [spec(*tail) for tail in tails]


def _prompt_layer(x, w, l, prev):
    batch, seq, _ = x.shape
    tile = PROMPT_TILE
    assert seq % tile == 0 and tile % CHUNK == 0 and tile >= PAST_A
    tok = pl.BlockSpec((1, tile, D_MODEL), lambda b, j: (b, j, 0))
    tails = [((n, batch, GROUP), lambda b, j: (0, 0, 0)) if step_major else ((1, n, GROUP), lambda b, j: (b, 0, 0))
             for n, _, step_major in _STATES]
    w_specs, w_args = _weight_args(w, l)
    return pl.pallas_call(
        functools.partial(_layer_kernel, sample=False, n_runs=1, seq_len=tile, n_prev=l),
        grid=(batch, seq // tile),
        in_specs=[tok] + w_specs + (_stacked_specs(l, tails) if prev else []),
        out_specs=[tok] + _stacked_specs(l + 1, tails),
        out_shape=[jax.ShapeDtypeStruct(x.shape, x.dtype)]
        + [jax.ShapeDtypeStruct((l + 1, n, batch, GROUP) if step_major else (l + 1, batch, n, GROUP), x.dtype)
           for n, _, step_major in _STATES],
        scratch_shapes=_scratch(1, tile),
        compiler_params=pltpu.CompilerParams(
            dimension_semantics=("arbitrary", "arbitrary"), vmem_limit_bytes=VMEM_LIMIT),
        name="prompt_layer",
    )(x, *w_args, *prev)


def _sample_layers(x2d, states, w, n_steps):
    batch = x2d.shape[0] // n_steps
    depth = states[0].shape[0]
    seqs = SAMPLE_SEQS
    assert batch % seqs == 0 and ROWS % n_steps == 0 and (seqs * n_steps) % CHUNK == 0 and CHUNK % n_steps == 0
    rows = seqs * n_steps
    state_specs = [pl.BlockSpec((1, n, seqs, GROUP), lambda l, i: (l, 0, i, 0)) if step_major
                   else pl.BlockSpec((1, seqs, n, GROUP), lambda l, i: (l, i, 0, 0))
                   for n, _, step_major in _STATES]
    tok = pl.BlockSpec((rows, D_MODEL), lambda l, i: (jnp.where(l == 0, i, batch // seqs - 1), 0))
    tok_out = pl.BlockSpec((rows, D_MODEL), lambda l, i: (jnp.where(l == depth - 1, i, 0), 0))
    w_specs, w_args = _weight_args(w, None)
    return pl.pallas_call(
        functools.partial(_layer_kernel, sample=True, n_runs=seqs, seq_len=n_steps, n_prev=0),
        grid=(depth, batch // seqs),
        in_specs=[tok] + state_specs + w_specs,
        out_specs=[tok_out] + state_specs + [pl.BlockSpec((1, rows, GROUP), lambda l, i: (l, i, 0))],
        out_shape=[jax.ShapeDtypeStruct(x2d.shape, x2d.dtype)]
        + [jax.ShapeDtypeStruct(s.shape, x2d.dtype) for s in states]
        + [jax.ShapeDtypeStruct((depth, batch * n_steps, GROUP), x2d.dtype)],
        scratch_shapes=_scratch(seqs, n_steps) + [pltpu.VMEM((batch // seqs, rows, D_MODEL), jnp.float32)],
        compiler_params=pltpu.CompilerParams(
            dimension_semantics=("arbitrary", "arbitrary"), vmem_limit_bytes=VMEM_LIMIT),
        name="sample_layers",
    )(x2d, *states, *w_args)


def _prepare_weights(pre_norm_g, w_in, conv_a_w, conv_a_b, ln_a_g, ln_a_b, conv_b_w, ln_c_g, ln_c_b,
                     spatial_w, spatial_b, pool_w, pool_scale, w_out, post_norm_g):
    depth = w_in.shape[0]
    eye = jnp.eye(N_HEADS, dtype=pool_w.dtype)
    pool_bd = jnp.einsum("lgcd,gh->lgchd", pool_w, eye).reshape(depth, GROUP, GROUP)
    return {
        "pre_g": pre_norm_g, "w_in": w_in.astype(jnp.bfloat16),
        "ca_w": jnp.swapaxes(conv_a_w, 0, 1), "ca_b": conv_a_b, "lna_g": ln_a_g, "lna_b": ln_a_b,
        "cb_w": jnp.swapaxes(conv_b_w, 0, 1), "lnc_g": ln_c_g, "lnc_b": ln_c_b,
        "sp_w": spatial_w, "sp_b": jnp.repeat(spatial_b.transpose(0, 2, 1), HEAD_DIM, axis=2),
        "pl_w": pool_bd.astype(jnp.bfloat16), "pl_s": pool_scale,
        "w_out": w_out.astype(jnp.bfloat16), "post_g": post_norm_g,
    }


def kernel(x_prompt, x_sample, state_conv_a, state_conv_b, state_pool, pre_norm_g, w_in, conv_a_w, conv_a_b, ln_a_g, ln_a_b, conv_b_w, ln_c_g, ln_c_b, spatial_w, spatial_b, pool_w, pool_scale, w_out, post_norm_g):
    depth = w_in.shape[0]
    dec_batch, n_steps, _ = x_sample.shape
    w = _prepare_weights(pre_norm_g, w_in, conv_a_w, conv_a_b, ln_a_g, ln_a_b, conv_b_w, ln_c_g,
                         ln_c_b, spatial_w, spatial_b, pool_w, pool_scale, w_out, post_norm_g)
    step_major = lambda v, flag: jnp.swapaxes(v, 1, 2) if flag else v
    states = [step_major(s, sm) for s, (_, _, sm) in zip((state_conv_a, state_conv_b, state_pool), _STATES)]
    yp, ys = x_prompt, x_sample.reshape(dec_batch * n_steps, D_MODEL)
    prompt_out = []
    for l in range(depth):
        yp, *prompt_out = _prompt_layer(yp, w, l, prompt_out)
    ys, *sample_out = _sample_layers(ys, states, w, n_steps)
    a_s, b_s, d_s = (step_major(s, sm) for s, (_, _, sm) in zip(sample_out[:3], _STATES))
    v_s = sample_out[3].reshape(depth, dec_batch, n_steps, GROUP)
    a_p, b_p, d_p = (step_major(s, sm) for s, (_, _, sm) in zip(prompt_out, _STATES))
    return (yp, ys.reshape(x_sample.shape), a_p, a_s, b_p, b_s, d_p, d_s, v_s)
```

```python
import functools
import types

import jax
import jax.numpy as jnp
from jax import lax
from jax.experimental import pallas as pl
from jax.experimental.pallas import tpu as pltpu

D_MODEL = 1024
GROUP = 256
HEAD_DIM = 64
N_HEADS = 4
CONV_A_WIDTH = 31
CONV_B_WIDTH = 3
CHUNK = 128
POOL_BUF = 15
EPS = 1e-6
LANES = 128
HALVES = GROUP // LANES

(A_VAL, A_GATE, Z_A, B_B, B_C, B_X, Z_B, C_U, C_V, Z_C, D_X, Z_D) = range(12)
N_IN = 12 * GROUP
SPATIAL_OUT = C_V
POOL_OUT = D_X

PAST_A = 32
PAST_B = 8
PAST_D = 16

PROMPT_TILE = 1024
SAMPLE_SEQS = 64
ROWS = 32

V7X_VMEM_BYTES = 64 * 1024 * 1024
VMEM_LIMIT = V7X_VMEM_BYTES - 6 * 1024 * 1024


def _sigmoid(v):
    return jax.nn.sigmoid(v)


def _silu(v):
    return v * _sigmoid(v)


def _rowsum(vs):
    tot = vs[0]
    for v in vs[1:]:
        tot = tot + v
    return jnp.sum(tot, axis=-1, keepdims=True)


def _half(h):
    return slice(h * LANES, (h + 1) * LANES)


def _slot(slot, h):
    return slice(slot * GROUP + h * LANES, slot * GROUP + (h + 1) * LANES)


def _layernorm2(v0, v1, g_ref, b_ref):
    mu = _rowsum([v0, v1]) * (1.0 / GROUP)
    d0, d1 = v0 - mu, v1 - mu
    var = _rowsum([d0 * d0, d1 * d1]) * (1.0 / GROUP)
    inv = lax.rsqrt(var + EPS)
    return (d0 * inv * g_ref[:, _half(0)] + b_ref[:, _half(0)],
            d1 * inv * g_ref[:, _half(1)] + b_ref[:, _half(1)])


class _Row:
    def __init__(self, ref, i, width):
        self.ref, self.i, self.width = ref, i, width

    def __getitem__(self, idx):
        lanes = slice(0, self.width) if idx is Ellipsis else idx[1]
        return self.ref[self.i:self.i + 1, lanes]


class _History:
    def __init__(self, ref, past, seq_len):
        self.ref, self.past, self.seq_len = ref, past, seq_len
        self.piece = min(seq_len, ROWS)

    def _offsets(self, r0):
        for i in range(ROWS // self.piece):
            run, step = divmod(r0 + i * self.piece, self.seq_len)
            yield i, run * (self.past + self.seq_len) + self.past + step

    def window(self, h, r0, back, extra=0):
        assert extra == 0 or self.piece == ROWS
        parts = [self.ref[h, off - back - extra:off - back + self.piece, :] for _, off in self._offsets(r0)]
        return parts[0] if len(parts) == 1 else jnp.concatenate(parts, axis=0)

    def store(self, h, r0, value):
        for i, off in self._offsets(r0):
            self.ref[h, off:off + self.piece, :] = value[i * self.piece:(i + 1) * self.piece]

    def tail(self, h, run, n):
        end = (run + 1) * (self.past + self.seq_len)
        return self.ref[h, end - n:end, :]

    def set_history(self, h, run, value):
        start = run * (self.past + self.seq_len) + self.past
        self.ref[h, start - value.shape[0]:start, :] = value

    def _across_runs(self, row, n_runs):
        return pl.ds(row, n_runs, stride=self.past + self.seq_len)

    def tail_step(self, h, n_runs, j, n):
        return self.ref.at[h][self._across_runs(self.past + self.seq_len - n + j, n_runs), :]

    def set_history_step(self, h, j, n, value):
        self.ref.at[h][self._across_runs(self.past - n + j, value.shape[0]), :] = value


def _phase_in(t, p):
    for r0 in range(0, t.n_rows, ROWS):
        xb = t.x[r0:r0 + ROWS, :]
        ms = jnp.sum(xb * xb, axis=-1, keepdims=True) * (1.0 / D_MODEL)
        hb = xb * lax.rsqrt(ms + EPS) * p.pre_g[...]
        t.mix[r0:r0 + ROWS, :] = hb.astype(jnp.bfloat16)
    t.proj[...] = jnp.dot(t.mix[...], p.w_in[...], preferred_element_type=jnp.float32)


def _phase_mix(t, p):
    lo = lax.broadcasted_iota(jnp.int32, (ROWS, LANES), 1) < HEAD_DIM
    inv_w = [jnp.where(lo, 1.0 / 2, 1.0 / 4), jnp.where(lo, 1.0 / 8, 1.0 / 16)]
    win = [jnp.where(lo, 2, 4), jnp.where(lo, 8, 16)]
    hist_a, hist_b, hist_d = t.hist_a, t.hist_b, t.hist_d

    def pj(r0, slot, h):
        return t.proj[r0:r0 + ROWS, _slot(slot, h)]

    for r0 in range(0, t.n_rows, ROWS):
        for h in range(HALVES):
            hist_a.store(h, r0, pj(r0, A_VAL, h) * _sigmoid(pj(r0, A_GATE, h)))
            hist_b.store(h, r0, pj(r0, B_C, h) * pj(r0, B_X, h))
            hist_d.store(h, r0, pj(r0, D_X, h))
        v0, v1 = _layernorm2(pj(r0, C_V, 0), pj(r0, C_V, 1), p.lnc_g, p.lnc_b)
        t.vn[r0:r0 + ROWS, _half(0)] = v0
        t.vn[r0:r0 + ROWS, _half(1)] = v1

        a2 = hist_d.window(0, r0, 0) + hist_d.window(0, r0, 1)
        a4 = a2 + (hist_d.window(0, r0, 2) + hist_d.window(0, r0, 3))
        sums = [jnp.where(lo, a2, a4)]
        if hist_d.piece == ROWS:
            s8 = hist_d.window(1, r0, 0, extra=8)
            for j in range(1, 8):
                s8 = s8 + hist_d.window(1, r0, j, extra=8)
            s8_cur, s8_prev = s8[8:], s8[:ROWS]
        else:
            s8_cur, s8_prev = hist_d.window(1, r0, 0), hist_d.window(1, r0, 8)
            for j in range(1, 8):
                s8_cur = s8_cur + hist_d.window(1, r0, j)
                s8_prev = s8_prev + hist_d.window(1, r0, 8 + j)
        sums.append(jnp.where(lo, s8_cur, s8_cur + s8_prev))
        cnt = t.first_block_cnt() if r0 == 0 else None
        for h in range(HALVES):
            if cnt is None:
                mean = sums[h] * inv_w[h]
            else:
                mean = sums[h] / jnp.minimum(cnt, win[h]).astype(jnp.float32)
            t.mix[r0:r0 + ROWS, _slot(3, h)] = (mean - hist_d.window(h, r0, 0)).astype(jnp.bfloat16)

    lo_chunk = lax.broadcasted_iota(jnp.int32, (CHUNK, LANES), 1) < HEAD_DIM
    for c0 in range(0, t.n_rows, CHUNK):
        for h in range(HALVES):
            vb = t.vn[c0:c0 + CHUNK, _half(h)].astype(jnp.bfloat16)
            s_even = jnp.dot(p.sp_w[2 * h], vb, preferred_element_type=jnp.float32)
            s_odd = jnp.dot(p.sp_w[2 * h + 1], vb, preferred_element_type=jnp.float32)
            t.proj[c0:c0 + CHUNK, _slot(SPATIAL_OUT, h)] = (
                jnp.where(lo_chunk, s_even, s_odd) + t.chunk_bias(h))
    t.proj[:, POOL_OUT * GROUP:(POOL_OUT + 1) * GROUP] = jnp.dot(
        t.mix[:, 3 * GROUP:4 * GROUP], p.pl_w[...], preferred_element_type=jnp.float32)

    for r0 in range(0, t.n_rows, ROWS):
        _mix_block(t, p, r0)
    t.proj[:, 0:D_MODEL] = jnp.dot(t.mix[...], p.w_out[...], preferred_element_type=jnp.float32)


def _mix_block(t, p, r0):
    hist_a, hist_b = t.hist_a, t.hist_b

    def pj(r0, slot, h):
        return t.proj[r0:r0 + ROWS, _slot(slot, h)]

    ya = []
    for h in range(HALVES):
        acc = jnp.broadcast_to(p.ca_b[:, _half(h)], (ROWS, LANES))
        for k in range(CONV_A_WIDTH):
            acc = acc + p.ca_w[k:k + 1, _half(h)] * hist_a.window(h, r0, CONV_A_WIDTH - 1 - k)
        ya.append(acc)
    na = _layernorm2(ya[0], ya[1], p.lna_g, p.lna_b)
    for h in range(HALVES):
        t.mix[r0:r0 + ROWS, _slot(0, h)] = (_silu(na[h]) * _silu(pj(r0, Z_A, h))).astype(jnp.bfloat16)

    for h in range(HALVES):
        acc = None
        for k in range(CONV_B_WIDTH):
            term = p.cb_w[k:k + 1, _half(h)] * hist_b.window(h, r0, CONV_B_WIDTH - 1 - k)
            acc = term if acc is None else acc + term
        yb = pj(r0, B_B, h) * acc * _silu(pj(r0, Z_B, h))
        t.mix[r0:r0 + ROWS, _slot(1, h)] = yb.astype(jnp.bfloat16)

    for h in range(HALVES):
        yc = pj(r0, C_U, h) * pj(r0, SPATIAL_OUT, h) * _silu(pj(r0, Z_C, h))
        t.mix[r0:r0 + ROWS, _slot(2, h)] = yc.astype(jnp.bfloat16)
        yd = pj(r0, POOL_OUT, h) * p.pl_s[:, _half(h)] * _silu(pj(r0, Z_D, h))
        t.mix[r0:r0 + ROWS, _slot(3, h)] = yd.astype(jnp.bfloat16)


def _phase_out(t, p):
    for r0 in range(0, t.n_rows, ROWS):
        ob = t.proj[r0:r0 + ROWS, 0:D_MODEL]
        ms = jnp.sum(ob * ob, axis=-1, keepdims=True) * (1.0 / D_MODEL)
        yb = t.x[r0:r0 + ROWS, :] + ob * lax.rsqrt(ms + EPS) * p.post_g[...]
        for y in t.ys:
            y[r0:r0 + ROWS, :] = yb


_WEIGHTS = ("pre_g", "w_in", "ca_w", "ca_b", "lna_g", "lna_b", "cb_w", "lnc_g", "lnc_b", "sp_w", "sp_b",
            "pl_w", "pl_s", "w_out", "post_g")
_STATES = ((CONV_A_WIDTH - 1, PAST_A, True), (CONV_B_WIDTH - 1, PAST_B, False), (POOL_BUF, PAST_D, True))


def _layer_kernel(*refs, sample, n_runs, seq_len, n_prev):
    n_state, n_w = len(_STATES), len(_WEIGHTS)
    n_stacked = n_state + (1 if sample else 0)
    n_carried = n_stacked if n_prev else 0
    n_in = 1 + (n_state if sample else 0) + n_w + n_carried
    n_out = 1 + n_stacked
    x_ref, state_in = refs[0], refs[1:n_in - n_w - n_carried]
    w = dict(zip(_WEIGHTS, refs[n_in - n_w - n_carried:n_in - n_carried]))
    carried = refs[n_in - n_carried:n_in]
    y_ref, stacked_out = refs[n_in], refs[n_in + 1:n_in + n_out]
    state_out = stacked_out[:n_state]
    scratch = refs[n_in + n_out:]
    sp_w, pl_w, vec_l, ca_w_l, cb_w_l, ext_a, ext_b, ext_d, proj, vn_ref, mix = scratch[:11]
    n_rows = n_runs * seq_len
    hists = [_History(ext, past, seq_len) for ext, (_, past, _) in zip((ext_a, ext_b, ext_d), _STATES)]

    if sample:
        first = pl.program_id(1) == 0
        tokens = scratch[11].at[pl.program_id(1)]

        @pl.when(pl.program_id(0) == 0)
        def _():
            tokens[...] = x_ref[...]
    else:
        j = pl.program_id(1)
        first = (pl.program_id(0) == 0) & (j == 0)

    @pl.when(first)
    def _():
        r_i = lax.broadcasted_iota(jnp.int32, (CHUNK, CHUNK), 0)
        c_i = lax.broadcasted_iota(jnp.int32, (CHUNK, CHUNK), 1)
        keep = c_i <= r_i
        if seq_len < CHUNK:
            keep = keep & (r_i // seq_len == c_i // seq_len)
            in_corner = lax.broadcasted_iota(jnp.int32, (seq_len, CHUNK), 1) < seq_len
        for hd in range(N_HEADS):
            mat = w["sp_w"][0, hd]
            if seq_len < CHUNK:
                corner = jnp.where(in_corner, mat[0:seq_len, :], 0.0)
                lanes = corner
                for q in range(1, CHUNK // seq_len):
                    lanes = lanes + pltpu.roll(corner, q * seq_len, axis=1)
                mat = jnp.concatenate([lanes] * (CHUNK // seq_len), axis=0)
            sp_w[hd] = jnp.where(keep, mat, 0.0).astype(jnp.bfloat16)
        for g in range(N_HEADS):
            row = [jnp.zeros((HEAD_DIM, HEAD_DIM), jnp.float32)] * N_HEADS
            row[g] = w["pl_w"][0, g]
            pl_w[g * HEAD_DIM:(g + 1) * HEAD_DIM, :] = jnp.concatenate(row, axis=1).astype(jnp.bfloat16)

    if sample:
        for hist, ref, (n, _, step_major) in zip(hists, state_in, _STATES):
            for h in range(HALVES):
                if step_major:
                    for s in range(n):
                        hist.set_history_step(h, s, n, ref[0, s, :, _half(h)])
                else:
                    for run in range(n_runs):
                        hist.set_history(h, run, ref[0, run, :, _half(h)])
        first_block_cnt = lambda: None
    else:
        @pl.when(j == 0)
        def _():
            for hist in hists:
                hist.ref[:, 0:hist.past, :] = jnp.zeros((HALVES, hist.past, LANES), jnp.float32)

        @pl.when(j > 0)
        def _():
            for hist in hists:
                hist.ref[:, 0:hist.past, :] = hist.ref[:, seq_len:seq_len + hist.past, :]

        def first_block_cnt():
            return j * seq_len + lax.broadcasted_iota(jnp.int32, (ROWS, LANES), 0) + 1

    layer = pl.program_id(0) if sample else n_prev
    p = types.SimpleNamespace(w_in=w["w_in"].at[0], w_out=w["w_out"].at[0], sp_w=sp_w, pl_w=pl_w,
                              **_layer_params(w, layer, vec_l, ca_w_l, cb_w_l))

    def chunk_bias(h):
        steps = min(seq_len, CHUNK)
        return jnp.concatenate([p.sp_b[0:steps, _half(h)]] * (CHUNK // steps), axis=0)

    t = types.SimpleNamespace(
        x=tokens if sample else x_ref.at[0], ys=(tokens, y_ref) if sample else (y_ref.at[0],),
        hist_a=hists[0], hist_b=hists[1], hist_d=hists[2], proj=proj, vn=vn_ref, mix=mix,
        n_rows=n_rows, first_block_cnt=first_block_cnt, chunk_bias=chunk_bias)
    _phase_in(t, p)
    _phase_mix(t, p)
    _phase_out(t, p)

    def write_stacked():
        for ref, src in zip(stacked_out, carried):
            for k in range(n_prev):
                ref[k] = src[k]
        for hist, ref, (n, _, step_major) in zip(hists, state_out, _STATES):
            for h in range(HALVES):
                if sample and step_major:
                    for s in range(n):
                        ref[n_prev, s, :, _half(h)] = hist.tail_step(h, n_runs, s, n)
                elif step_major:
                    row0 = hist.past + seq_len - n
                    for b in range(ref.shape[2]):
                        @pl.when(pl.program_id(0) == b)
                        def _(b=b, ref=ref, hist=hist, h=h, row0=row0, n=n):
                            for s in range(n):
                                ref[n_prev, s, b:b + 1, _half(h)] = hist.ref[h, row0 + s:row0 + s + 1, :]
                else:
                    for run in range(n_runs):
                        ref[n_prev, run, :, _half(h)] = hist.tail(h, run, n)
        if sample:
            stacked_out[n_state][n_prev] = vn_ref[...]

    if sample:
        write_stacked()
    else:
        pl.when(j == pl.num_programs(1) - 1)(write_stacked)


def _layer_params(w, layer, vec_l, ca_w_l, cb_w_l):
    def of_layer(get):
        if isinstance(layer, int):
            return get(layer)
        val = get(0)
        for k in range(1, w["pre_g"].shape[0]):
            val = jnp.where(layer == k, get(k), val)
        return val

    rows = {}
    for i, name in enumerate(_ROW_VECTORS):
        width = w[name].shape[1]
        vec_l[i:i + 1, 0:width] = of_layer(lambda k, name=name: w[name][k:k + 1, :])
        rows[name] = _Row(vec_l, i, width)
    ca_w_l[...] = of_layer(lambda k: w["ca_w"][:, k, :])
    cb_w_l[...] = of_layer(lambda k: w["cb_w"][:, k, :])
    return dict(rows, ca_w=ca_w_l, cb_w=cb_w_l, sp_b=w["sp_b"].at[0])


def _layer_spec(arr, l, single_buffer=False):
    shape = (1,) + arr.shape[1:]
    index_map = lambda *g: (g[0] if l is None else l,) + (0,) * (len(shape) - 1)
    if single_buffer:
        return pl.BlockSpec(shape, index_map, pipeline_mode=pl.Buffered(1))
    return pl.BlockSpec(shape, index_map)


def _small_scratch():
    return [
        pltpu.VMEM((len(_ROW_VECTORS), D_MODEL), jnp.float32),
        pltpu.VMEM((CONV_A_WIDTH, GROUP), jnp.float32),
        pltpu.VMEM((CONV_B_WIDTH, GROUP), jnp.float32),
    ]


def _tile_scratch(n_runs, seq_len):
    n_rows = n_runs * seq_len
    return [
        pltpu.VMEM((HALVES, n_runs * (past + seq_len), LANES), jnp.float32) for _, past, _ in _STATES
    ] + [
        pltpu.VMEM((n_rows, N_IN), jnp.float32),
        pltpu.VMEM((n_rows, GROUP), jnp.float32),
        pltpu.VMEM((n_rows, D_MODEL), jnp.bfloat16),
    ]


def _scratch(n_runs, seq_len):
    return ([pltpu.VMEM((N_HEADS, CHUNK, CHUNK), jnp.bfloat16), pltpu.VMEM((GROUP, GROUP), jnp.bfloat16)]
            + _small_scratch()
            + _tile_scratch(n_runs, seq_len))


_LAYER_BLOCKS = ("w_in", "w_out", "sp_w", "sp_b", "pl_w")
_ROW_VECTORS = ("pre_g", "post_g", "ca_b", "lna_g", "lna_b", "lnc_g", "lnc_b", "pl_s")


def _weight_args(w, l):
    def spec(n):
        if n in _LAYER_BLOCKS:
            return _layer_spec(w[n], l, single_buffer=l is not None and n in ("w_in", "w_out"))
        return pl.BlockSpec(w[n].shape, lambda *g: (0,) * w[n].ndim)

    return [spec(n) for n in _WEIGHTS], [w[n] for n in _WEIGHTS]


def _stacked_specs(n_layers, tails):
    def spec(block_tail, tail_index):
        return pl.BlockSpec((n_layers,) + block_tail, lambda *g: (0,) + tail_index(*g))

    return [spec(*tail) for tail in tails]


def _prompt_layer(x, w, l, prev):
    batch, seq, _ = x.shape
    tile = PROMPT_TILE
    assert seq % tile == 0 and tile % CHUNK == 0 and tile >= PAST_A
    tok = pl.BlockSpec((1, tile, D_MODEL), lambda b, j: (b, j, 0))
    tails = [((n, batch, GROUP), lambda b, j: (0, 0, 0)) if step_major else ((1, n, GROUP), lambda b, j: (b, 0, 0))
             for n, _, step_major in _STATES]
    w_specs, w_args = _weight_args(w, l)
    return pl.pallas_call(
        functools.partial(_layer_kernel, sample=False, n_runs=1, seq_len=tile, n_prev=l),
        grid=(batch, seq // tile),
        in_specs=[tok] + w_specs + (_stacked_specs(l, tails) if prev else []),
        out_specs=[tok] + _stacked_specs(l + 1, tails),
        out_shape=[jax.ShapeDtypeStruct(x.shape, x.dtype)]
        + [jax.ShapeDtypeStruct((l + 1, n, batch, GROUP) if step_major else (l + 1, batch, n, GROUP), x.dtype)
           for n, _, step_major in _STATES],
        scratch_shapes=_scratch(1, tile),
        compiler_params=pltpu.CompilerParams(
            dimension_semantics=("arbitrary", "arbitrary"), vmem_limit_bytes=VMEM_LIMIT),
        name="prompt_layer",
    )(x, *w_args, *prev)


def _sample_layers(x2d, states, w, n_steps):
    batch = x2d.shape[0] // n_steps
    depth = states[0].shape[0]
    seqs = SAMPLE_SEQS
    assert batch % seqs == 0 and ROWS % n_steps == 0 and (seqs * n_steps) % CHUNK == 0 and CHUNK % n_steps == 0
    rows = seqs * n_steps
    state_specs = [pl.BlockSpec((1, n, seqs, GROUP), lambda l, i: (l, 0, i, 0)) if step_major
                   else pl.BlockSpec((1, seqs, n, GROUP), lambda l, i: (l, i, 0, 0))
                   for n, _, step_major in _STATES]
    tok = pl.BlockSpec((rows, D_MODEL), lambda l, i: (jnp.where(l == 0, i, batch // seqs - 1), 0))
    tok_out = pl.BlockSpec((rows, D_MODEL), lambda l, i: (jnp.where(l == depth - 1, i, 0), 0))
    w_specs, w_args = _weight_args(w, None)
    return pl.pallas_call(
        functools.partial(_layer_kernel, sample=True, n_runs=seqs, seq_len=n_steps, n_prev=0),
        grid=(depth, batch // seqs),
        in_specs=[tok] + state_specs + w_specs,
        out_specs=[tok_out] + state_specs + [pl.BlockSpec((1, rows, GROUP), lambda l, i: (l, i, 0))],
        out_shape=[jax.ShapeDtypeStruct(x2d.shape, x2d.dtype)]
        + [jax.ShapeDtypeStruct(s.shape, x2d.dtype) for s in states]
        + [jax.ShapeDtypeStruct((depth, batch * n_steps, GROUP), x2d.dtype)],
        scratch_shapes=_scratch(seqs, n_steps) + [pltpu.VMEM((batch // seqs, rows, D_MODEL), jnp.float32)],
        compiler_params=pltpu.CompilerParams(
            dimension_semantics=("arbitrary", "arbitrary"), vmem_limit_bytes=VMEM_LIMIT),
        name="sample_layers",
    )(x2d, *states, *w_args)


def _prepare_weights(pre_norm_g, w_in, conv_a_w, conv_a_b, ln_a_g, ln_a_b, conv_b_w, ln_c_g, ln_c_b,
                     spatial_w, spatial_b, pool_w, pool_scale, w_out, post_norm_g):
    return {
        "pre_g": pre_norm_g, "w_in": w_in.astype(jnp.bfloat16),
        "ca_w": jnp.swapaxes(conv_a_w, 0, 1), "ca_b": conv_a_b, "lna_g": ln_a_g, "lna_b": ln_a_b,
        "cb_w": jnp.swapaxes(conv_b_w, 0, 1), "lnc_g": ln_c_g, "lnc_b": ln_c_b,
        "sp_w": spatial_w, "sp_b": jnp.repeat(spatial_b.transpose(0, 2, 1), HEAD_DIM, axis=2),
        "pl_w": pool_w, "pl_s": pool_scale,
        "w_out": w_out.astype(jnp.bfloat16), "post_g": post_norm_g,
    }


def kernel(x_prompt, x_sample, state_conv_a, state_conv_b, state_pool, pre_norm_g, w_in, conv_a_w, conv_a_b, ln_a_g, ln_a_b, conv_b_w, ln_c_g, ln_c_b, spatial_w, spatial_b, pool_w, pool_scale, w_out, post_norm_g):
    depth = w_in.shape[0]
    dec_batch, n_steps, _ = x_sample.shape
    w = _prepare_weights(pre_norm_g, w_in, conv_a_w, conv_a_b, ln_a_g, ln_a_b, conv_b_w, ln_c_g,
                         ln_c_b, spatial_w, spatial_b, pool_w, pool_scale, w_out, post_norm_g)
    step_major = lambda v, flag: jnp.swapaxes(v, 1, 2) if flag else v
    states = [step_major(s, sm) for s, (_, _, sm) in zip((state_conv_a, state_conv_b, state_pool), _STATES)]
    yp, ys = x_prompt, x_sample.reshape(dec_batch * n_steps, D_MODEL)
    prompt_out = []
    for l in range(depth):
        yp, *prompt_out = _prompt_layer(yp, w, l, prompt_out)
    ys, *sample_out = _sample_layers(ys, states, w, n_steps)
    a_s, b_s, d_s = (step_major(s, sm) for s, (_, _, sm) in zip(sample_out[:3], _STATES))
    v_s = sample_out[3].reshape(depth, dec_batch, n_steps, GROUP)
    a_p, b_p, d_p = (step_major(s, sm) for s, (_, _, sm) in zip(prompt_out, _STATES))
    return (yp, ys.reshape(x_sample.shape), a_p, a_s, b_p, b_s, d_p, d_s, v_s)
```

```python
import functools
import types

import jax
import jax.numpy as jnp
from jax import lax
from jax.experimental import pallas as pl
from jax.experimental.pallas import tpu as pltpu

D_MODEL = 1024
GROUP = 256
HEAD_DIM = 64
N_HEADS = 4
CONV_A_WIDTH = 31
CONV_B_WIDTH = 3
CHUNK = 128
POOL_WINDOWS = (2, 4, 8, 16)
POOL_BUF = max(POOL_WINDOWS) - 1
PAST_LEN = 16384
EPS = 1e-6
LANES = 128
HALVES = GROUP // LANES

(A_VAL, A_GATE, Z_A, B_B, B_C, B_X, Z_B, C_U, C_V, Z_C, D_X, Z_D) = range(12)
N_IN = 12 * GROUP
SPATIAL_OUT = C_V
POOL_OUT = D_X

PAST_A = 32
PAST_B = 8
PAST_D = 16

PROMPT_TILE = 1024
SAMPLE_SEQS = 64
ROWS = 32

V7X_VMEM_BYTES = 64 * 1024 * 1024
VMEM_LIMIT = V7X_VMEM_BYTES - 6 * 1024 * 1024


def _sigmoid(v):
    return jax.nn.sigmoid(v)


def _silu(v):
    return v * _sigmoid(v)


def _rowsum(vs):
    tot = vs[0]
    for v in vs[1:]:
        tot = tot + v
    return jnp.sum(tot, axis=-1, keepdims=True)


def _half(h):
    return slice(h * LANES, (h + 1) * LANES)


def _slot(slot, h):
    return slice(slot * GROUP + h * LANES, slot * GROUP + (h + 1) * LANES)


def _layernorm2(v0, v1, g_ref, b_ref):
    mu = _rowsum([v0, v1]) * (1.0 / GROUP)
    d0, d1 = v0 - mu, v1 - mu
    var = _rowsum([d0 * d0, d1 * d1]) * (1.0 / GROUP)
    inv = lax.rsqrt(var + EPS)
    return (d0 * inv * g_ref[:, _half(0)] + b_ref[:, _half(0)],
            d1 * inv * g_ref[:, _half(1)] + b_ref[:, _half(1)])


class _Row:
    def __init__(self, ref, i, width):
        self.ref, self.i, self.width = ref, i, width

    def __getitem__(self, idx):
        lanes = slice(0, self.width) if idx is Ellipsis else idx[1]
        return self.ref[self.i:self.i + 1, lanes]


class _History:
    def __init__(self, ref, past, seq_len):
        self.ref, self.past, self.seq_len = ref, past, seq_len
        self.piece = min(seq_len, ROWS)

    def _offsets(self, r0):
        for i in range(ROWS // self.piece):
            run, step = divmod(r0 + i * self.piece, self.seq_len)
            yield i, run * (self.past + self.seq_len) + self.past + step

    def window(self, h, r0, back, extra=0):
        assert extra == 0 or self.piece == ROWS
        parts = [self.ref[h, off - back - extra:off - back + self.piece, :] for _, off in self._offsets(r0)]
        return parts[0] if len(parts) == 1 else jnp.concatenate(parts, axis=0)

    def store(self, h, r0, value):
        for i, off in self._offsets(r0):
            self.ref[h, off:off + self.piece, :] = value[i * self.piece:(i + 1) * self.piece]

    def tail(self, h, run, n):
        end = (run + 1) * (self.past + self.seq_len)
        return self.ref[h, end - n:end, :]

    def set_history(self, h, run, value):
        start = run * (self.past + self.seq_len) + self.past
        self.ref[h, start - value.shape[0]:start, :] = value

    def _across_runs(self, row, n_runs):
        return pl.ds(row, n_runs, stride=self.past + self.seq_len)

    def tail_step(self, h, n_runs, j, n):
        return self.ref.at[h][self._across_runs(self.past + self.seq_len - n + j, n_runs), :]

    def set_history_step(self, h, j, n, value):
        self.ref.at[h][self._across_runs(self.past - n + j, value.shape[0]), :] = value


def _phase_in(t, p):
    for r0 in range(0, t.n_rows, ROWS):
        xb = t.x[r0:r0 + ROWS, :]
        ms = jnp.sum(xb * xb, axis=-1, keepdims=True) * (1.0 / D_MODEL)
        hb = xb * lax.rsqrt(ms + EPS) * p.pre_g[...]
        t.mix[r0:r0 + ROWS, :] = hb.astype(jnp.bfloat16)
    t.proj[...] = jnp.dot(t.mix[...], p.w_in[...], preferred_element_type=jnp.float32)


def _phase_mix(t, p):
    lo = lax.broadcasted_iota(jnp.int32, (ROWS, LANES), 1) < HEAD_DIM
    inv_w = [jnp.where(lo, 1.0 / POOL_WINDOWS[2 * h], 1.0 / POOL_WINDOWS[2 * h + 1]) for h in range(HALVES)]
    win = [jnp.where(lo, POOL_WINDOWS[2 * h], POOL_WINDOWS[2 * h + 1]) for h in range(HALVES)]
    hist_a, hist_b, hist_d = t.hist_a, t.hist_b, t.hist_d

    def pj(r0, slot, h):
        return t.proj[r0:r0 + ROWS, _slot(slot, h)]

    for r0 in range(0, t.n_rows, ROWS):
        for h in range(HALVES):
            hist_a.store(h, r0, pj(r0, A_VAL, h) * _sigmoid(pj(r0, A_GATE, h)))
            hist_b.store(h, r0, pj(r0, B_C, h) * pj(r0, B_X, h))
            hist_d.store(h, r0, pj(r0, D_X, h))
        v0, v1 = _layernorm2(pj(r0, C_V, 0), pj(r0, C_V, 1), p.lnc_g, p.lnc_b)
        t.vn[r0:r0 + ROWS, _half(0)] = v0
        t.vn[r0:r0 + ROWS, _half(1)] = v1

        a2 = hist_d.window(0, r0, 0) + hist_d.window(0, r0, 1)
        a4 = a2 + (hist_d.window(0, r0, 2) + hist_d.window(0, r0, 3))
        sums = [jnp.where(lo, a2, a4)]
        if hist_d.piece == ROWS:
            s8 = hist_d.window(1, r0, 0, extra=8)
            for j in range(1, 8):
                s8 = s8 + hist_d.window(1, r0, j, extra=8)
            s8_cur, s8_prev = s8[8:], s8[:ROWS]
        else:
            s8_cur, s8_prev = hist_d.window(1, r0, 0), hist_d.window(1, r0, 8)
            for j in range(1, 8):
                s8_cur = s8_cur + hist_d.window(1, r0, j)
                s8_prev = s8_prev + hist_d.window(1, r0, 8 + j)
        sums.append(jnp.where(lo, s8_cur, s8_cur + s8_prev))
        cnt = t.first_block_cnt() if r0 == 0 else None
        for h in range(HALVES):
            if cnt is None:
                mean = sums[h] * inv_w[h]
            else:
                mean = sums[h] / jnp.minimum(cnt, win[h]).astype(jnp.float32)
            t.mix[r0:r0 + ROWS, _slot(3, h)] = (mean - hist_d.window(h, r0, 0)).astype(jnp.bfloat16)

    lo_chunk = lax.broadcasted_iota(jnp.int32, (CHUNK, LANES), 1) < HEAD_DIM
    for c0 in range(0, t.n_rows, CHUNK):
        for h in range(HALVES):
            vb = t.vn[c0:c0 + CHUNK, _half(h)].astype(jnp.bfloat16)
            s_even = jnp.dot(p.sp_w[2 * h], vb, preferred_element_type=jnp.float32)
            s_odd = jnp.dot(p.sp_w[2 * h + 1], vb, preferred_element_type=jnp.float32)
            t.proj[c0:c0 + CHUNK, _slot(SPATIAL_OUT, h)] = (
                jnp.where(lo_chunk, s_even, s_odd) + t.chunk_bias(h))
    t.proj[:, POOL_OUT * GROUP:(POOL_OUT + 1) * GROUP] = jnp.dot(
        t.mix[:, 3 * GROUP:4 * GROUP], p.pl_w[...], preferred_element_type=jnp.float32)

    for r0 in range(0, t.n_rows, ROWS):
        _mix_block(t, p, r0)
    t.proj[:, 0:D_MODEL] = jnp.dot(t.mix[...], p.w_out[...], preferred_element_type=jnp.float32)


def _mix_block(t, p, r0):
    hist_a, hist_b = t.hist_a, t.hist_b

    def pj(r0, slot, h):
        return t.proj[r0:r0 + ROWS, _slot(slot, h)]

    ya = []
    for h in range(HALVES):
        acc = jnp.broadcast_to(p.ca_b[:, _half(h)], (ROWS, LANES))
        for k in range(CONV_A_WIDTH):
            acc = acc + p.ca_w[k:k + 1, _half(h)] * hist_a.window(h, r0, CONV_A_WIDTH - 1 - k)
        ya.append(acc)
    na = _layernorm2(ya[0], ya[1], p.lna_g, p.lna_b)
    for h in range(HALVES):
        t.mix[r0:r0 + ROWS, _slot(0, h)] = (_silu(na[h]) * _silu(pj(r0, Z_A, h))).astype(jnp.bfloat16)

    for h in range(HALVES):
        acc = None
        for k in range(CONV_B_WIDTH):
            term = p.cb_w[k:k + 1, _half(h)] * hist_b.window(h, r0, CONV_B_WIDTH - 1 - k)
            acc = term if acc is None else acc + term
        yb = pj(r0, B_B, h) * acc * _silu(pj(r0, Z_B, h))
        t.mix[r0:r0 + ROWS, _slot(1, h)] = yb.astype(jnp.bfloat16)

    for h in range(HALVES):
        yc = pj(r0, C_U, h) * pj(r0, SPATIAL_OUT, h) * _silu(pj(r0, Z_C, h))
        t.mix[r0:r0 + ROWS, _slot(2, h)] = yc.astype(jnp.bfloat16)
        yd = pj(r0, POOL_OUT, h) * p.pl_s[:, _half(h)] * _silu(pj(r0, Z_D, h))
        t.mix[r0:r0 + ROWS, _slot(3, h)] = yd.astype(jnp.bfloat16)


def _phase_out(t, p):
    for r0 in range(0, t.n_rows, ROWS):
        ob = t.proj[r0:r0 + ROWS, 0:D_MODEL]
        ms = jnp.sum(ob * ob, axis=-1, keepdims=True) * (1.0 / D_MODEL)
        yb = t.x[r0:r0 + ROWS, :] + ob * lax.rsqrt(ms + EPS) * p.post_g[...]
        for y in t.ys:
            y[r0:r0 + ROWS, :] = yb


_WEIGHTS = ("pre_g", "w_in", "ca_w", "ca_b", "lna_g", "lna_b", "cb_w", "lnc_g", "lnc_b", "sp_w", "sp_b",
            "pl_w", "pl_s", "w_out", "post_g")
_STATES = ((CONV_A_WIDTH - 1, PAST_A, True), (CONV_B_WIDTH - 1, PAST_B, False), (POOL_BUF, PAST_D, True))


def _layer_kernel(*refs, sample, n_runs, seq_len, n_prev):
    n_state, n_w = len(_STATES), len(_WEIGHTS)
    n_stacked = n_state + (1 if sample else 0)
    n_carried = n_stacked if n_prev else 0
    n_in = 1 + (n_state if sample else 0) + n_w + n_carried
    n_out = 1 + n_stacked
    x_ref, state_in = refs[0], refs[1:n_in - n_w - n_carried]
    w = dict(zip(_WEIGHTS, refs[n_in - n_w - n_carried:n_in - n_carried]))
    carried = refs[n_in - n_carried:n_in]
    y_ref, stacked_out = refs[n_in], refs[n_in + 1:n_in + n_out]
    state_out = stacked_out[:n_state]
    scratch = refs[n_in + n_out:]
    sp_w, pl_w, vec_l, ca_w_l, cb_w_l, ext_a, ext_b, ext_d, proj, vn_ref, mix = scratch[:11]
    n_rows = n_runs * seq_len
    hists = [_History(ext, past, seq_len) for ext, (_, past, _) in zip((ext_a, ext_b, ext_d), _STATES)]

    if sample:
        first = pl.program_id(1) == 0
        tokens = scratch[11].at[pl.program_id(1)]

        @pl.when(pl.program_id(0) == 0)
        def _():
            tokens[...] = x_ref[...]
    else:
        j = pl.program_id(1)
        first = (pl.program_id(0) == 0) & (j == 0)

    @pl.when(first)
    def _():
        r_i = lax.broadcasted_iota(jnp.int32, (CHUNK, CHUNK), 0)
        c_i = lax.broadcasted_iota(jnp.int32, (CHUNK, CHUNK), 1)
        keep = c_i <= r_i
        if seq_len < CHUNK:
            keep = keep & (r_i // seq_len == c_i // seq_len)
            in_corner = lax.broadcasted_iota(jnp.int32, (seq_len, CHUNK), 1) < seq_len
        for hd in range(N_HEADS):
            mat = w["sp_w"][0, hd]
            if seq_len < CHUNK:
                corner = jnp.where(in_corner, mat[0:seq_len, :], 0.0)
                lanes = corner
                for q in range(1, CHUNK // seq_len):
                    lanes = lanes + pltpu.roll(corner, q * seq_len, axis=1)
                mat = jnp.concatenate([lanes] * (CHUNK // seq_len), axis=0)
            sp_w[hd] = jnp.where(keep, mat, 0.0).astype(jnp.bfloat16)
        for g in range(N_HEADS):
            row = [jnp.zeros((HEAD_DIM, HEAD_DIM), jnp.float32)] * N_HEADS
            row[g] = w["pl_w"][0, g]
            pl_w[g * HEAD_DIM:(g + 1) * HEAD_DIM, :] = jnp.concatenate(row, axis=1).astype(jnp.bfloat16)

    if sample:
        for hist, ref, (n, _, step_major) in zip(hists, state_in, _STATES):
            for h in range(HALVES):
                if step_major:
                    for s in range(n):
                        hist.set_history_step(h, s, n, ref[0, s, :, _half(h)])
                else:
                    for run in range(n_runs):
                        hist.set_history(h, run, ref[0, run, :, _half(h)])
        first_block_cnt = lambda: None
    else:
        @pl.when(j == 0)
        def _():
            for hist in hists:
                hist.ref[:, 0:hist.past, :] = jnp.zeros((HALVES, hist.past, LANES), jnp.float32)

        @pl.when(j > 0)
        def _():
            for hist in hists:
                hist.ref[:, 0:hist.past, :] = hist.ref[:, seq_len:seq_len + hist.past, :]

        def first_block_cnt():
            return j * seq_len + lax.broadcasted_iota(jnp.int32, (ROWS, LANES), 0) + 1

    layer = pl.program_id(0) if sample else n_prev
    p = types.SimpleNamespace(w_in=w["w_in"].at[0], w_out=w["w_out"].at[0], sp_w=sp_w, pl_w=pl_w,
                              **_layer_params(w, layer, vec_l, ca_w_l, cb_w_l))

    def chunk_bias(h):
        steps = min(seq_len, CHUNK)
        return jnp.concatenate([p.sp_b[0:steps, _half(h)]] * (CHUNK // steps), axis=0)

    t = types.SimpleNamespace(
        x=tokens if sample else x_ref.at[0], ys=(tokens, y_ref) if sample else (y_ref.at[0],),
        hist_a=hists[0], hist_b=hists[1], hist_d=hists[2], proj=proj, vn=vn_ref, mix=mix,
        n_rows=n_rows, first_block_cnt=first_block_cnt, chunk_bias=chunk_bias)
    _phase_in(t, p)
    _phase_mix(t, p)
    _phase_out(t, p)

    def write_stacked():
        for ref, src in zip(stacked_out, carried):
            for k in range(n_prev):
                ref[k] = src[k]
        for hist, ref, (n, _, step_major) in zip(hists, state_out, _STATES):
            for h in range(HALVES):
                if sample and step_major:
                    for s in range(n):
                        ref[n_prev, s, :, _half(h)] = hist.tail_step(h, n_runs, s, n)
                elif step_major:
                    row0 = hist.past + seq_len - n
                    for b in range(ref.shape[2]):
                        @pl.when(pl.program_id(0) == b)
                        def _(b=b, ref=ref, hist=hist, h=h, row0=row0, n=n):
                            for s in range(n):
                                ref[n_prev, s, b:b + 1, _half(h)] = hist.ref[h, row0 + s:row0 + s + 1, :]
                else:
                    for run in range(n_runs):
                        ref[n_prev, run, :, _half(h)] = hist.tail(h, run, n)
        if sample:
            stacked_out[n_state][n_prev] = vn_ref[...]

    if sample:
        write_stacked()
    else:
        pl.when(j == pl.num_programs(1) - 1)(write_stacked)


def _layer_params(w, layer, vec_l, ca_w_l, cb_w_l):
    def of_layer(get):
        if isinstance(layer, int):
            return get(layer)
        val = get(0)
        for k in range(1, w["pre_g"].shape[0]):
            val = jnp.where(layer == k, get(k), val)
        return val

    rows = {}
    for i, name in enumerate(_ROW_VECTORS):
        width = w[name].shape[1]
        vec_l[i:i + 1, 0:width] = of_layer(lambda k, name=name: w[name][k:k + 1, :])
        rows[name] = _Row(vec_l, i, width)
    ca_w_l[...] = of_layer(lambda k: w["ca_w"][:, k, :])
    cb_w_l[...] = of_layer(lambda k: w["cb_w"][:, k, :])
    return dict(rows, ca_w=ca_w_l, cb_w=cb_w_l, sp_b=w["sp_b"].at[0])


def _layer_spec(arr, l, single_buffer=False):
    shape = (1,) + arr.shape[1:]
    index_map = lambda *g: (g[0] if l is None else l,) + (0,) * (len(shape) - 1)
    if single_buffer:
        return pl.BlockSpec(shape, index_map, pipeline_mode=pl.Buffered(1))
    return pl.BlockSpec(shape, index_map)


def _small_scratch():
    return [
        pltpu.VMEM((len(_ROW_VECTORS), D_MODEL), jnp.float32),
        pltpu.VMEM((CONV_A_WIDTH, GROUP), jnp.float32),
        pltpu.VMEM((CONV_B_WIDTH, GROUP), jnp.float32),
    ]


def _tile_scratch(n_runs, seq_len):
    n_rows = n_runs * seq_len
    return [
        pltpu.VMEM((HALVES, n_runs * (past + seq_len), LANES), jnp.float32) for _, past, _ in _STATES
    ] + [
        pltpu.VMEM((n_rows, N_IN), jnp.float32),
        pltpu.VMEM((n_rows, GROUP), jnp.float32),
        pltpu.VMEM((n_rows, D_MODEL), jnp.bfloat16),
    ]


def _scratch(n_runs, seq_len):
    return ([pltpu.VMEM((N_HEADS, CHUNK, CHUNK), jnp.bfloat16), pltpu.VMEM((GROUP, GROUP), jnp.bfloat16)]
            + _small_scratch()
            + _tile_scratch(n_runs, seq_len))


_LAYER_BLOCKS = ("w_in", "w_out", "sp_w", "sp_b", "pl_w")
_ROW_VECTORS = ("pre_g", "post_g", "ca_b", "lna_g", "lna_b", "lnc_g", "lnc_b", "pl_s")


def _weight_args(w, l):
    def spec(n):
        if n in _LAYER_BLOCKS:
            return _layer_spec(w[n], l, single_buffer=l is not None and n in ("w_in", "w_out"))
        return pl.BlockSpec(w[n].shape, lambda *g: (0,) * w[n].ndim)

    return [spec(n) for n in _WEIGHTS], [w[n] for n in _WEIGHTS]


def _stacked_specs(n_layers, tails):
    def spec(block_tail, tail_index):
        return pl.BlockSpec((n_layers,) + block_tail, lambda *g: (0,) + tail_index(*g))

    return [spec(*tail) for tail in tails]


def _prompt_layer(x, w, l, prev):
    batch, seq, _ = x.shape
    tile = PROMPT_TILE
    assert seq % tile == 0 and tile % CHUNK == 0 and tile >= PAST_A
    tok = pl.BlockSpec((1, tile, D_MODEL), lambda b, j: (b, j, 0))
    tails = [((n, batch, GROUP), lambda b, j: (0, 0, 0)) if step_major else ((1, n, GROUP), lambda b, j: (b, 0, 0))
             for n, _, step_major in _STATES]
    w_specs, w_args = _weight_args(w, l)
    return pl.pallas_call(
        functools.partial(_layer_kernel, sample=False, n_runs=1, seq_len=tile, n_prev=l),
        grid=(batch, seq // tile),
        in_specs=[tok] + w_specs + (_stacked_specs(l, tails) if prev else []),
        out_specs=[tok] + _stacked_specs(l + 1, tails),
        out_shape=[jax.ShapeDtypeStruct(x.shape, x.dtype)]
        + [jax.ShapeDtypeStruct((l + 1, n, batch, GROUP) if step_major else (l + 1, batch, n, GROUP), x.dtype)
           for n, _, step_major in _STATES],
        scratch_shapes=_scratch(1, tile),
        compiler_params=pltpu.CompilerParams(
            dimension_semantics=("arbitrary", "arbitrary"), vmem_limit_bytes=VMEM_LIMIT),
        name="prompt_layer",
    )(x, *w_args, *prev)


def _sample_layers(x2d, states, w, n_steps):
    batch = x2d.shape[0] // n_steps
    depth = states[0].shape[0]
    seqs = SAMPLE_SEQS
    assert batch % seqs == 0 and ROWS % n_steps == 0 and (seqs * n_steps) % CHUNK == 0 and CHUNK % n_steps == 0
    rows = seqs * n_steps
    state_specs = [pl.BlockSpec((1, n, seqs, GROUP), lambda l, i: (l, 0, i, 0)) if step_major
                   else pl.BlockSpec((1, seqs, n, GROUP), lambda l, i: (l, i, 0, 0))
                   for n, _, step_major in _STATES]
    tok = pl.BlockSpec((rows, D_MODEL), lambda l, i: (jnp.where(l == 0, i, batch // seqs - 1), 0))
    tok_out = pl.BlockSpec((rows, D_MODEL), lambda l, i: (jnp.where(l == depth - 1, i, 0), 0))
    w_specs, w_args = _weight_args(w, None)
    return pl.pallas_call(
        functools.partial(_layer_kernel, sample=True, n_runs=seqs, seq_len=n_steps, n_prev=0),
        grid=(depth, batch // seqs),
        in_specs=[tok] + state_specs + w_specs,
        out_specs=[tok_out] + state_specs + [pl.BlockSpec((1, rows, GROUP), lambda l, i: (l, i, 0))],
        out_shape=[jax.ShapeDtypeStruct(x2d.shape, x2d.dtype)]
        + [jax.ShapeDtypeStruct(s.shape, x2d.dtype) for s in states]
        + [jax.ShapeDtypeStruct((depth, batch * n_steps, GROUP), x2d.dtype)],
        scratch_shapes=_scratch(seqs, n_steps) + [pltpu.VMEM((batch // seqs, rows, D_MODEL), jnp.float32)],
        compiler_params=pltpu.CompilerParams(
            dimension_semantics=("arbitrary", "arbitrary"), vmem_limit_bytes=VMEM_LIMIT),
        name="sample_layers",
    )(x2d, *states, *w_args)


def _prepare_weights(pre_norm_g, w_in, conv_a_w, conv_a_b, ln_a_g, ln_a_b, conv_b_w, ln_c_g, ln_c_b,
                     spatial_w, spatial_b, pool_w, pool_scale, w_out, post_norm_g):
    return {
        "pre_g": pre_norm_g, "w_in": w_in.astype(jnp.bfloat16),
        "ca_w": jnp.swapaxes(conv_a_w, 0, 1), "ca_b": conv_a_b, "lna_g": ln_a_g, "lna_b": ln_a_b,
        "cb_w": jnp.swapaxes(conv_b_w, 0, 1), "lnc_g": ln_c_g, "lnc_b": ln_c_b,
        "sp_w": spatial_w, "sp_b": jnp.repeat(spatial_b.transpose(0, 2, 1), HEAD_DIM, axis=2),
        "pl_w": pool_w, "pl_s": pool_scale,
        "w_out": w_out.astype(jnp.bfloat16), "post_g": post_norm_g,
    }


def kernel(x_prompt, x_sample, state_conv_a, state_conv_b, state_pool, pre_norm_g, w_in, conv_a_w, conv_a_b, ln_a_g, ln_a_b, conv_b_w, ln_c_g, ln_c_b, spatial_w, spatial_b, pool_w, pool_scale, w_out, post_norm_g):
    depth = w_in.shape[0]
    dec_batch, n_steps, _ = x_sample.shape
    assert POOL_WINDOWS == (2, 4, 8, 16) and PAST_LEN % CHUNK == 0 and PAST_LEN >= POOL_BUF and n_steps <= CHUNK
    assert x_prompt.shape[2] == D_MODEL == x_sample.shape[2] and w_in.shape[1:] == (D_MODEL, N_IN)
    w = _prepare_weights(pre_norm_g, w_in, conv_a_w, conv_a_b, ln_a_g, ln_a_b, conv_b_w, ln_c_g,
                         ln_c_b, spatial_w, spatial_b, pool_w, pool_scale, w_out, post_norm_g)
    step_major = lambda v, flag: jnp.swapaxes(v, 1, 2) if flag else v
    states = [step_major(s, sm) for s, (_, _, sm) in zip((state_conv_a, state_conv_b, state_pool), _STATES)]
    yp, ys = x_prompt, x_sample.reshape(dec_batch * n_steps, D_MODEL)
    prompt_out = []
    for l in range(depth):
        yp, *prompt_out = _prompt_layer(yp, w, l, prompt_out)
    ys, *sample_out = _sample_layers(ys, states, w, n_steps)
    a_s, b_s, d_s = (step_major(s, sm) for s, (_, _, sm) in zip(sample_out[:3], _STATES))
    v_s = sample_out[3].reshape(depth, dec_batch, n_steps, GROUP)
    a_p, b_p, d_p = (step_major(s, sm) for s, (_, _, sm) in zip(prompt_out, _STATES))
    return (yp, ys.reshape(x_sample.shape), a_p, a_s, b_p, b_s, d_p, d_s, v_s)
```

```python
import functools
import types

import jax
import jax.numpy as jnp
from jax import lax
from jax.experimental import pallas as pl
from jax.experimental.pallas import tpu as pltpu

D_MODEL = 1024
GROUP = 256
HEAD_DIM = 64
N_HEADS = 4
CONV_A_WIDTH = 31
CONV_B_WIDTH = 3
CHUNK = 128
POOL_WINDOWS = (2, 4, 8, 16)
POOL_BUF = max(POOL_WINDOWS) - 1
PAST_LEN = 16384
EPS = 1e-6
LANES = 128
HALVES = GROUP // LANES

(A_VAL, A_GATE, Z_A, B_B, B_C, B_X, Z_B, C_U, C_V, Z_C, D_X, Z_D) = range(12)
N_IN = 12 * GROUP
SPATIAL_OUT = C_V
POOL_OUT = D_X

PAST_A = 32
PAST_B = 8
PAST_D = 16

PROMPT_TILE = 1024
SAMPLE_SEQS = 64
ROWS = 32

V7X_VMEM_BYTES = 64 * 1024 * 1024
VMEM_LIMIT = V7X_VMEM_BYTES - 6 * 1024 * 1024


def _sigmoid(v):
    return jax.nn.sigmoid(v)


def _silu(v):
    return v * _sigmoid(v)


def _rowsum(vs):
    tot = vs[0]
    for v in vs[1:]:
        tot = tot + v
    return jnp.sum(tot, axis=-1, keepdims=True)


def _half(h):
    return slice(h * LANES, (h + 1) * LANES)


def _slot(slot, h):
    return slice(slot * GROUP + h * LANES, slot * GROUP + (h + 1) * LANES)


def _layernorm2(v0, v1, g_ref, b_ref):
    mu = _rowsum([v0, v1]) * (1.0 / GROUP)
    d0, d1 = v0 - mu, v1 - mu
    var = _rowsum([d0 * d0, d1 * d1]) * (1.0 / GROUP)
    inv = lax.rsqrt(var + EPS)
    return (d0 * inv * g_ref[:, _half(0)] + b_ref[:, _half(0)],
            d1 * inv * g_ref[:, _half(1)] + b_ref[:, _half(1)])


class _Row:
    def __init__(self, ref, i, width):
        self.ref, self.i, self.width = ref, i, width

    def __getitem__(self, idx):
        lanes = slice(0, self.width) if idx is Ellipsis else idx[1]
        return self.ref[self.i:self.i + 1, lanes]


class _History:
    def __init__(self, ref, past, seq_len):
        self.ref, self.past, self.seq_len = ref, past, seq_len
        self.piece = min(seq_len, ROWS)

    def _offsets(self, r0):
        for i in range(ROWS // self.piece):
            run, step = divmod(r0 + i * self.piece, self.seq_len)
            yield i, run * (self.past + self.seq_len) + self.past + step

    def window(self, h, r0, back, extra=0):
        assert extra == 0 or self.piece == ROWS
        parts = [self.ref[h, off - back - extra:off - back + self.piece, :] for _, off in self._offsets(r0)]
        return parts[0] if len(parts) == 1 else jnp.concatenate(parts, axis=0)

    def store(self, h, r0, value):
        for i, off in self._offsets(r0):
            self.ref[h, off:off + self.piece, :] = value[i * self.piece:(i + 1) * self.piece]

    def tail(self, h, run, n):
        end = (run + 1) * (self.past + self.seq_len)
        return self.ref[h, end - n:end, :]

    def set_history(self, h, run, value):
        start = run * (self.past + self.seq_len) + self.past
        self.ref[h, start - value.shape[0]:start, :] = value

    def _across_runs(self, row, n_runs):
        return pl.ds(row, n_runs, stride=self.past + self.seq_len)

    def tail_step(self, h, n_runs, j, n):
        return self.ref.at[h][self._across_runs(self.past + self.seq_len - n + j, n_runs), :]

    def set_history_step(self, h, j, n, value):
        self.ref.at[h][self._across_runs(self.past - n + j, value.shape[0]), :] = value


def _phase_in(t, p):
    for r0 in range(0, t.n_rows, ROWS):
        xb = t.x[r0:r0 + ROWS, :]
        ms = jnp.sum(xb * xb, axis=-1, keepdims=True) * (1.0 / D_MODEL)
        hb = xb * lax.rsqrt(ms + EPS) * p.pre_g[...]
        t.mix[r0:r0 + ROWS, :] = hb.astype(jnp.bfloat16)
    t.proj[...] = jnp.dot(t.mix[...], p.w_in[...], preferred_element_type=jnp.float32)


def _phase_mix(t, p):
    lo = lax.broadcasted_iota(jnp.int32, (ROWS, LANES), 1) < HEAD_DIM
    inv_w = [jnp.where(lo, 1.0 / POOL_WINDOWS[2 * h], 1.0 / POOL_WINDOWS[2 * h + 1]) for h in range(HALVES)]
    win = [jnp.where(lo, POOL_WINDOWS[2 * h], POOL_WINDOWS[2 * h + 1]) for h in range(HALVES)]
    hist_a, hist_b, hist_d = t.hist_a, t.hist_b, t.hist_d

    def pj(r0, slot, h):
        return t.proj[r0:r0 + ROWS, _slot(slot, h)]

    for r0 in range(0, t.n_rows, ROWS):
        for h in range(HALVES):
            hist_a.store(h, r0, pj(r0, A_VAL, h) * _sigmoid(pj(r0, A_GATE, h)))
            hist_b.store(h, r0, pj(r0, B_C, h) * pj(r0, B_X, h))
            hist_d.store(h, r0, pj(r0, D_X, h))
        v0, v1 = _layernorm2(pj(r0, C_V, 0), pj(r0, C_V, 1), p.lnc_g, p.lnc_b)
        t.vn[r0:r0 + ROWS, _half(0)] = v0
        t.vn[r0:r0 + ROWS, _half(1)] = v1

        a2 = hist_d.window(0, r0, 0) + hist_d.window(0, r0, 1)
        a4 = a2 + (hist_d.window(0, r0, 2) + hist_d.window(0, r0, 3))
        sums = [jnp.where(lo, a2, a4)]
        if hist_d.piece == ROWS:
            s8 = hist_d.window(1, r0, 0, extra=8)
            for j in range(1, 8):
                s8 = s8 + hist_d.window(1, r0, j, extra=8)
            s8_cur, s8_prev = s8[8:], s8[:ROWS]
        else:
            s8_cur, s8_prev = hist_d.window(1, r0, 0), hist_d.window(1, r0, 8)
            for j in range(1, 8):
                s8_cur = s8_cur + hist_d.window(1, r0, j)
                s8_prev = s8_prev + hist_d.window(1, r0, 8 + j)
        sums.append(jnp.where(lo, s8_cur, s8_cur + s8_prev))
        cnt = t.first_block_cnt() if r0 == 0 else None
        for h in range(HALVES):
            if cnt is None:
                mean = sums[h] * inv_w[h]
            else:
                mean = sums[h] / jnp.minimum(cnt, win[h]).astype(jnp.float32)
            t.mix[r0:r0 + ROWS, _slot(3, h)] = (mean - hist_d.window(h, r0, 0)).astype(jnp.bfloat16)

    lo_chunk = lax.broadcasted_iota(jnp.int32, (CHUNK, LANES), 1) < HEAD_DIM
    for c0 in range(0, t.n_rows, CHUNK):
        for h in range(HALVES):
            vb = t.vn[c0:c0 + CHUNK, _half(h)].astype(jnp.bfloat16)
            s_even = jnp.dot(p.sp_w[2 * h], vb, preferred_element_type=jnp.float32)
            s_odd = jnp.dot(p.sp_w[2 * h + 1], vb, preferred_element_type=jnp.float32)
            t.proj[c0:c0 + CHUNK, _slot(SPATIAL_OUT, h)] = (
                jnp.where(lo_chunk, s_even, s_odd) + t.chunk_bias(h))
    t.proj[:, POOL_OUT * GROUP:(POOL_OUT + 1) * GROUP] = jnp.dot(
        t.mix[:, 3 * GROUP:4 * GROUP], p.pl_w[...], preferred_element_type=jnp.float32)

    for r0 in range(0, t.n_rows, ROWS):
        _mix_block(t, p, r0)
    t.proj[:, 0:D_MODEL] = jnp.dot(t.mix[...], p.w_out[...], preferred_element_type=jnp.float32)


def _mix_block(t, p, r0):
    hist_a, hist_b = t.hist_a, t.hist_b

    def pj(r0, slot, h):
        return t.proj[r0:r0 + ROWS, _slot(slot, h)]

    ya = []
    for h in range(HALVES):
        acc = jnp.broadcast_to(p.ca_b[:, _half(h)], (ROWS, LANES))
        for k in range(CONV_A_WIDTH):
            acc = acc + p.ca_w[k:k + 1, _half(h)] * hist_a.window(h, r0, CONV_A_WIDTH - 1 - k)
        ya.append(acc)
    na = _layernorm2(ya[0], ya[1], p.lna_g, p.lna_b)
    for h in range(HALVES):
        t.mix[r0:r0 + ROWS, _slot(0, h)] = (_silu(na[h]) * _silu(pj(r0, Z_A, h))).astype(jnp.bfloat16)

    for h in range(HALVES):
        acc = None
        for k in range(CONV_B_WIDTH):
            term = p.cb_w[k:k + 1, _half(h)] * hist_b.window(h, r0, CONV_B_WIDTH - 1 - k)
            acc = term if acc is None else acc + term
        yb = pj(r0, B_B, h) * acc * _silu(pj(r0, Z_B, h))
        t.mix[r0:r0 + ROWS, _slot(1, h)] = yb.astype(jnp.bfloat16)

    for h in range(HALVES):
        yc = pj(r0, C_U, h) * pj(r0, SPATIAL_OUT, h) * _silu(pj(r0, Z_C, h))
        t.mix[r0:r0 + ROWS, _slot(2, h)] = yc.astype(jnp.bfloat16)
        yd = pj(r0, POOL_OUT, h) * p.pl_s[:, _half(h)] * _silu(pj(r0, Z_D, h))
        t.mix[r0:r0 + ROWS, _slot(3, h)] = yd.astype(jnp.bfloat16)


def _phase_out(t, p):
    for r0 in range(0, t.n_rows, ROWS):
        ob = t.proj[r0:r0 + ROWS, 0:D_MODEL]
        ms = jnp.sum(ob * ob, axis=-1, keepdims=True) * (1.0 / D_MODEL)
        yb = t.x[r0:r0 + ROWS, :] + ob * lax.rsqrt(ms + EPS) * p.post_g[...]
        for y in t.ys:
            y[r0:r0 + ROWS, :] = yb


_WEIGHTS = ("pre_g", "w_in", "ca_w", "ca_b", "lna_g", "lna_b", "cb_w", "lnc_g", "lnc_b", "sp_w", "sp_b",
            "pl_w", "pl_s", "w_out", "post_g")
_STATES = ((CONV_A_WIDTH - 1, PAST_A, True), (CONV_B_WIDTH - 1, PAST_B, False), (POOL_BUF, PAST_D, True))


def _layer_kernel(*refs, sample, n_runs, seq_len, n_prev, grid_ids=None, grid_shape=None):
    n_state, n_w = len(_STATES), len(_WEIGHTS)
    n_stacked = n_state + (1 if sample else 0)
    n_carried = n_stacked if n_prev else 0
    n_in = 1 + (n_state if sample else 0) + n_w + n_carried
    n_out = 1 + n_stacked
    x_ref, state_in = refs[0], refs[1:n_in - n_w - n_carried]
    w = dict(zip(_WEIGHTS, refs[n_in - n_w - n_carried:n_in - n_carried]))
    carried = refs[n_in - n_carried:n_in]
    y_ref, stacked_out = refs[n_in], refs[n_in + 1:n_in + n_out]
    state_out = stacked_out[:n_state]
    scratch = refs[n_in + n_out:]
    sp_w, pl_w, vec_l, ca_w_l, cb_w_l, ext_a, ext_b, ext_d, proj, vn_ref, mix = scratch[:11]
    n_rows = n_runs * seq_len
    hists = [_History(ext, past, seq_len) for ext, (_, past, _) in zip((ext_a, ext_b, ext_d), _STATES)]
    pid = pl.program_id if grid_ids is None else (lambda axis: grid_ids[axis])
    last_tile = (pl.num_programs(1) if grid_shape is None else grid_shape[1]) - 1

    if sample:
        first = pid(1) == 0
        tokens = scratch[11].at[pid(1)]

        @pl.when(pid(0) == 0)
        def _():
            tokens[...] = x_ref[...]
    else:
        j = pid(1)
        first = (pid(0) == 0) & (j == 0)

    @pl.when(first)
    def _():
        r_i = lax.broadcasted_iota(jnp.int32, (CHUNK, CHUNK), 0)
        c_i = lax.broadcasted_iota(jnp.int32, (CHUNK, CHUNK), 1)
        keep = c_i <= r_i
        if seq_len < CHUNK:
            keep = keep & (r_i // seq_len == c_i // seq_len)
            in_corner = lax.broadcasted_iota(jnp.int32, (seq_len, CHUNK), 1) < seq_len
        for hd in range(N_HEADS):
            mat = w["sp_w"][0, hd]
            if seq_len < CHUNK:
                corner = jnp.where(in_corner, mat[0:seq_len, :], 0.0)
                lanes = corner
                for q in range(1, CHUNK // seq_len):
                    lanes = lanes + pltpu.roll(corner, q * seq_len, axis=1)
                mat = jnp.concatenate([lanes] * (CHUNK // seq_len), axis=0)
            sp_w[hd] = jnp.where(keep, mat, 0.0).astype(jnp.bfloat16)
        for g in range(N_HEADS):
            row = [jnp.zeros((HEAD_DIM, HEAD_DIM), jnp.float32)] * N_HEADS
            row[g] = w["pl_w"][0, g]
            pl_w[g * HEAD_DIM:(g + 1) * HEAD_DIM, :] = jnp.concatenate(row, axis=1).astype(jnp.bfloat16)

    if sample:
        for hist, ref, (n, _, step_major) in zip(hists, state_in, _STATES):
            for h in range(HALVES):
                if step_major:
                    for s in range(n):
                        hist.set_history_step(h, s, n, ref[0, s, :, _half(h)])
                else:
                    for run in range(n_runs):
                        hist.set_history(h, run, ref[0, run, :, _half(h)])
        first_block_cnt = lambda: None
    else:
        @pl.when(j == 0)
        def _():
            for hist in hists:
                hist.ref[:, 0:hist.past, :] = jnp.zeros((HALVES, hist.past, LANES), jnp.float32)

        @pl.when(j > 0)
        def _():
            for hist in hists:
                hist.ref[:, 0:hist.past, :] = hist.ref[:, seq_len:seq_len + hist.past, :]

        def first_block_cnt():
            return j * seq_len + lax.broadcasted_iota(jnp.int32, (ROWS, LANES), 0) + 1

    layer = pid(0) if sample else n_prev
    p = types.SimpleNamespace(w_in=w["w_in"].at[0], w_out=w["w_out"].at[0], sp_w=sp_w, pl_w=pl_w,
                              **_layer_params(w, layer, vec_l, ca_w_l, cb_w_l))

    def chunk_bias(h):
        steps = min(seq_len, CHUNK)
        return jnp.concatenate([p.sp_b[0:steps, _half(h)]] * (CHUNK // steps), axis=0)

    t = types.SimpleNamespace(
        x=tokens if sample else x_ref.at[0], ys=(tokens, y_ref) if sample else (y_ref.at[0],),
        hist_a=hists[0], hist_b=hists[1], hist_d=hists[2], proj=proj, vn=vn_ref, mix=mix,
        n_rows=n_rows, first_block_cnt=first_block_cnt, chunk_bias=chunk_bias)
    _phase_in(t, p)
    _phase_mix(t, p)
    _phase_out(t, p)

    def write_stacked():
        for ref, src in zip(stacked_out, carried):
            for k in range(n_prev):
                ref[k] = src[k]
        for hist, ref, (n, _, step_major) in zip(hists, state_out, _STATES):
            for h in range(HALVES):
                if sample and step_major:
                    for s in range(n):
                        ref[n_prev, s, :, _half(h)] = hist.tail_step(h, n_runs, s, n)
                elif step_major:
                    row0 = hist.past + seq_len - n
                    for b in range(ref.shape[2]):
                        @pl.when(pid(0) == b)
                        def _(b=b, ref=ref, hist=hist, h=h, row0=row0, n=n):
                            for s in range(n):
                                ref[n_prev, s, b:b + 1, _half(h)] = hist.ref[h, row0 + s:row0 + s + 1, :]
                else:
                    for run in range(n_runs):
                        ref[n_prev, run, :, _half(h)] = hist.tail(h, run, n)
        if sample:
            stacked_out[n_state][n_prev] = vn_ref[...]

    if sample:
        write_stacked()
    else:
        pl.when(j == last_tile)(write_stacked)


def _layer_params(w, layer, vec_l, ca_w_l, cb_w_l):
    def of_layer(get):
        if isinstance(layer, int):
            return get(layer)
        val = get(0)
        for k in range(1, w["pre_g"].shape[0]):
            val = jnp.where(layer == k, get(k), val)
        return val

    rows = {}
    for i, name in enumerate(_ROW_VECTORS):
        width = w[name].shape[1]
        vec_l[i:i + 1, 0:width] = of_layer(lambda k, name=name: w[name][k:k + 1, :])
        rows[name] = _Row(vec_l, i, width)
    ca_w_l[...] = of_layer(lambda k: w["ca_w"][:, k, :])
    cb_w_l[...] = of_layer(lambda k: w["cb_w"][:, k, :])
    return dict(rows, ca_w=ca_w_l, cb_w=cb_w_l, sp_b=w["sp_b"].at[0])


def _layer_spec(arr, l, single_buffer=False):
    shape = (1,) + arr.shape[1:]
    index_map = lambda *g: (g[0] if l is None else l,) + (0,) * (len(shape) - 1)
    if single_buffer:
        return pl.BlockSpec(shape, index_map, pipeline_mode=pl.Buffered(1))
    return pl.BlockSpec(shape, index_map)


def _small_scratch():
    return [
        pltpu.VMEM((len(_ROW_VECTORS), D_MODEL), jnp.float32),
        pltpu.VMEM((CONV_A_WIDTH, GROUP), jnp.float32),
        pltpu.VMEM((CONV_B_WIDTH, GROUP), jnp.float32),
    ]


def _tile_scratch(n_runs, seq_len):
    n_rows = n_runs * seq_len
    return [
        pltpu.VMEM((HALVES, n_runs * (past + seq_len), LANES), jnp.float32) for _, past, _ in _STATES
    ] + [
        pltpu.VMEM((n_rows, N_IN), jnp.float32),
        pltpu.VMEM((n_rows, GROUP), jnp.float32),
        pltpu.VMEM((n_rows, D_MODEL), jnp.bfloat16),
    ]


def _scratch(n_runs, seq_len):
    return ([pltpu.VMEM((N_HEADS, CHUNK, CHUNK), jnp.bfloat16), pltpu.VMEM((GROUP, GROUP), jnp.bfloat16)]
            + _small_scratch()
            + _tile_scratch(n_runs, seq_len))


_LAYER_BLOCKS = ("w_in", "w_out", "sp_w", "sp_b", "pl_w")
_ROW_VECTORS = ("pre_g", "post_g", "ca_b", "lna_g", "lna_b", "lnc_g", "lnc_b", "pl_s")


def _weight_args(w, l, single_buffer=True):
    def spec(n):
        if n in _LAYER_BLOCKS:
            return _layer_spec(w[n], l, single_buffer=single_buffer and l is not None and n in ("w_in", "w_out"))
        return pl.BlockSpec(w[n].shape, lambda *g: (0,) * w[n].ndim)

    return [spec(n) for n in _WEIGHTS], [w[n] for n in _WEIGHTS]


def _stacked_specs(n_layers, tails):
    def spec(block_tail, tail_index):
        return pl.BlockSpec((n_layers,) + block_tail, lambda *g: (0,) + tail_index(*g))

    return [spec(*tail) for tail in tails]


def _prompt_layer(x, w, l, prev):
    batch, seq, _ = x.shape
    tile = PROMPT_TILE
    assert seq % tile == 0 and tile % CHUNK == 0 and tile >= PAST_A
    tok = pl.BlockSpec((1, tile, D_MODEL), lambda b, j: (b, j, 0))
    tails = [((n, batch, GROUP), lambda b, j: (0, 0, 0)) if step_major else ((1, n, GROUP), lambda b, j: (b, 0, 0))
             for n, _, step_major in _STATES]
    w_specs, w_args = _weight_args(w, l, single_buffer=False)
    grid = (batch, seq // tile)
    in_specs = [tok] + w_specs + (_stacked_specs(l, tails) if prev else [])
    out_specs = [tok] + _stacked_specs(l + 1, tails)
    n_in, n_out = len(in_specs), len(out_specs)

    def call(*refs):
        operands, scratch = refs[:n_in + n_out], refs[n_in + n_out:]

        def body(*blocks):
            _layer_kernel(*blocks, *scratch, sample=False, n_runs=1, seq_len=tile, n_prev=l, grid_shape=grid)

        pltpu.emit_pipeline(body, grid=grid, in_specs=in_specs, out_specs=out_specs)(*operands)

    hbm = pl.BlockSpec(memory_space=pl.ANY)
    return pl.pallas_call(
        call,
        in_specs=[hbm] * n_in,
        out_specs=[hbm] * n_out,
        out_shape=[jax.ShapeDtypeStruct(x.shape, x.dtype)]
        + [jax.ShapeDtypeStruct((l + 1, n, batch, GROUP) if step_major else (l + 1, batch, n, GROUP), x.dtype)
           for n, _, step_major in _STATES],
        scratch_shapes=_scratch(1, tile),
        compiler_params=pltpu.CompilerParams(vmem_limit_bytes=VMEM_LIMIT),
        name="prompt_layer",
    )(x, *w_args, *prev)


def _sample_layers(x2d, states, w, n_steps):
    batch = x2d.shape[0] // n_steps
    depth = states[0].shape[0]
    seqs = SAMPLE_SEQS
    assert batch % seqs == 0 and ROWS % n_steps == 0 and (seqs * n_steps) % CHUNK == 0 and CHUNK % n_steps == 0
    rows = seqs * n_steps
    state_specs = [pl.BlockSpec((1, n, seqs, GROUP), lambda l, i: (l, 0, i, 0)) if step_major
                   else pl.BlockSpec((1, seqs, n, GROUP), lambda l, i: (l, i, 0, 0))
                   for n, _, step_major in _STATES]
    tok = pl.BlockSpec((rows, D_MODEL), lambda l, i: (jnp.where(l == 0, i, batch // seqs - 1), 0))
    tok_out = pl.BlockSpec((rows, D_MODEL), lambda l, i: (jnp.where(l == depth - 1, i, 0), 0))
    w_specs, w_args = _weight_args(w, None)
    return pl.pallas_call(
        functools.partial(_layer_kernel, sample=True, n_runs=seqs, seq_len=n_steps, n_prev=0),
        grid=(depth, batch // seqs),
        in_specs=[tok] + state_specs + w_specs,
        out_specs=[tok_out] + state_specs + [pl.BlockSpec((1, rows, GROUP), lambda l, i: (l, i, 0))],
        out_shape=[jax.ShapeDtypeStruct(x2d.shape, x2d.dtype)]
        + [jax.ShapeDtypeStruct(s.shape, x2d.dtype) for s in states]
        + [jax.ShapeDtypeStruct((depth, batch * n_steps, GROUP), x2d.dtype)],
        scratch_shapes=_scratch(seqs, n_steps) + [pltpu.VMEM((batch // seqs, rows, D_MODEL), jnp.float32)],
        compiler_params=pltpu.CompilerParams(
            dimension_semantics=("arbitrary", "arbitrary"), vmem_limit_bytes=VMEM_LIMIT),
        name="sample_layers",
    )(x2d, *states, *w_args)


def _prepare_weights(pre_norm_g, w_in, conv_a_w, conv_a_b, ln_a_g, ln_a_b, conv_b_w, ln_c_g, ln_c_b,
                     spatial_w, spatial_b, pool_w, pool_scale, w_out, post_norm_g):
    return {
        "pre_g": pre_norm_g, "w_in": w_in.astype(jnp.bfloat16),
        "ca_w": jnp.swapaxes(conv_a_w, 0, 1), "ca_b": conv_a_b, "lna_g": ln_a_g, "lna_b": ln_a_b,
        "cb_w": jnp.swapaxes(conv_b_w, 0, 1), "lnc_g": ln_c_g, "lnc_b": ln_c_b,
        "sp_w": spatial_w, "sp_b": jnp.repeat(spatial_b.transpose(0, 2, 1), HEAD_DIM, axis=2),
        "pl_w": pool_w, "pl_s": pool_scale,
        "w_out": w_out.astype(jnp.bfloat16), "post_g": post_norm_g,
    }


def kernel(x_prompt, x_sample, state_conv_a, state_conv_b, state_pool, pre_norm_g, w_in, conv_a_w, conv_a_b, ln_a_g, ln_a_b, conv_b_w, ln_c_g, ln_c_b, spatial_w, spatial_b, pool_w, pool_scale, w_out, post_norm_g):
    depth = w_in.shape[0]
    dec_batch, n_steps, _ = x_sample.shape
    assert POOL_WINDOWS == (2, 4, 8, 16) and PAST_LEN % CHUNK == 0 and PAST_LEN >= POOL_BUF and n_steps <= CHUNK
    assert x_prompt.shape[2] == D_MODEL == x_sample.shape[2] and w_in.shape[1:] == (D_MODEL, N_IN)
    w = _prepare_weights(pre_norm_g, w_in, conv_a_w, conv_a_b, ln_a_g, ln_a_b, conv_b_w, ln_c_g,
                         ln_c_b, spatial_w, spatial_b, pool_w, pool_scale, w_out, post_norm_g)
    step_major = lambda v, flag: jnp.swapaxes(v, 1, 2) if flag else v
    states = [step_major(s, sm) for s, (_, _, sm) in zip((state_conv_a, state_conv_b, state_pool), _STATES)]
    yp, ys = x_prompt, x_sample.reshape(dec_batch * n_steps, D_MODEL)
    prompt_out = []
    for l in range(depth):
        yp, *prompt_out = _prompt_layer(yp, w, l, prompt_out)
    ys, *sample_out = _sample_layers(ys, states, w, n_steps)
    a_s, b_s, d_s = (step_major(s, sm) for s, (_, _, sm) in zip(sample_out[:3], _STATES))
    v_s = sample_out[3].reshape(depth, dec_batch, n_steps, GROUP)
    a_p, b_p, d_p = (step_major(s, sm) for s, (_, _, sm) in zip(prompt_out, _STATES))
    return (yp, ys.reshape(x_sample.shape), a_p, a_s, b_p, b_s, d_p, d_s, v_s)
```
